```python
import math
import jax, jax.numpy as jnp
from jax import lax
import numpy as np

D_MODEL = 1024
BATCH = 8
SEQ = 2048
DEPTH = 2
DEC_BATCH = 128
DEC_SEQ = 1
PAST_LEN = 16384
PAGE_SIZE = 128

N_MIXERS = 2
N_RET_LAYERS = (DEPTH + 1) // 2
N_MLP_LAYERS = DEPTH // 2
RET_HEADS = 4
RET_DK = D_MODEL // RET_HEADS
RET_DV = 2 * RET_DK
RET_QKW = RET_HEADS * RET_DK
RET_VW = RET_HEADS * RET_DV
RET_CHUNK = 128
ROPE_BASE = 10000.0
MLP_WIDTH = 2 * D_MODEL
MLP_GROUPS = 8
MLP_GW = MLP_WIDTH // MLP_GROUPS
MLP_CHUNK = 128
ALPHA = (2 * DEPTH) ** 0.25
BETA = (8 * DEPTH) ** -0.25
LN_EPS = 1e-5

kernel_name = 'retnet_gmlp_interleaved_step'


def _layer_norm(x, gain, bias):
    xf = x.astype(jnp.float32)
    mu = jnp.mean(xf, -1, keepdims=True)
    var = jnp.mean(jnp.square(xf - mu), -1, keepdims=True)
    y = (xf - mu) * lax.rsqrt(var + LN_EPS) * gain.astype(jnp.float32) + bias.astype(jnp.float32)
    return y.astype(x.dtype)


def _rotary(x, pos):
    half = x.shape[-1] // 2
    inv = ROPE_BASE ** (-jnp.arange(half, dtype=jnp.float32) / half)
    ang = pos.astype(jnp.float32)[:, None] * inv[None, :]
    cos = jnp.cos(ang)[None, :, None, :]
    sin = jnp.sin(ang)[None, :, None, :]
    x1, x2 = x[..., :half], x[..., half:]
    return jnp.concatenate([x1 * cos - x2 * sin, x2 * cos + x1 * sin], axis=-1)


def _log_gamma():
    return jnp.log1p(-jnp.exp2(-5.0 - jnp.arange(RET_HEADS, dtype=jnp.float32)))


def _retention_block(q, k, v, s0, lg):
    L = q.shape[1]
    idx = jnp.arange(L, dtype=jnp.float32)
    diff = idx[:, None] - idx[None, :]
    decay = jnp.where(diff >= 0, jnp.exp(jnp.maximum(diff, 0.0)[None] * lg[:, None, None]), 0.0)
    scores = jnp.einsum('bihd,bjhd->bhij', q, k) * decay[None]
    o = jnp.einsum('bhij,bjhe->bihe', scores, v)
    cross = jnp.exp((idx[:, None] + 1.0) * lg[None, :])
    o = o + jnp.einsum('bihd,bhde->bihe', q, s0) * cross[None, :, :, None]
    k_dec = k * jnp.exp((L - 1.0 - idx)[:, None] * lg[None, :])[None, :, :, None]
    s1 = jnp.exp(L * lg)[None, :, None, None] * s0 + jnp.einsum('bjhd,bjhe->bhde', k_dec, v)
    return o, s1


def _retention_mixer(x, s0, pos, w_in, gn_gain, w_out):
    B, L, _ = x.shape
    h = x @ w_in
    q, k, v, g = jnp.split(h, [RET_QKW, 2 * RET_QKW, 2 * RET_QKW + RET_VW], axis=-1)
    q = _rotary(q.reshape(B, L, RET_HEADS, RET_DK).astype(jnp.float32), pos)
    k = _rotary(k.reshape(B, L, RET_HEADS, RET_DK).astype(jnp.float32), pos) * (RET_DK ** -0.5)
    v = v.reshape(B, L, RET_HEADS, RET_DV).astype(jnp.float32)
    lg = _log_gamma()
    chunk = min(L, RET_CHUNK)
    nc = L // chunk

    def to_chunks(t):
        return jnp.moveaxis(t.reshape(B, nc, chunk, RET_HEADS, t.shape[-1]), 1, 0)

    def step(s, qkv):
        qc, kc, vc = qkv
        o, s = _retention_block(qc, kc, vc, s, lg)
        return s, o

    s1, o = lax.scan(step, s0.astype(jnp.float32), (to_chunks(q), to_chunks(k), to_chunks(v)))
    o = jnp.moveaxis(o, 0, 1).reshape(B, L, RET_HEADS, RET_DV)
    mu = jnp.mean(o, -1, keepdims=True)
    var = jnp.mean(jnp.square(o - mu), -1, keepdims=True)
    o = ((o - mu) * lax.rsqrt(var + LN_EPS)).reshape(B, L, RET_VW) * gn_gain.astype(jnp.float32)
    y = (jax.nn.silu(g.astype(jnp.float32)) * o).astype(x.dtype) @ w_out
    return y, s1.astype(x.dtype)


def _chunk_mlp_mixer(x, w_in, ln_g, ln_b, w_s, b_s, w_out):
    B, L, _ = x.shape
    chunk = min(L, MLP_CHUNK)
    nc = L // chunk
    u, v, g = jnp.split(x @ w_in, 3, axis=-1)
    u = jax.nn.gelu(u, approximate=False)
    v = _layer_norm(jax.nn.gelu(v, approximate=False), ln_g, ln_b)
    causal = jnp.tril(jnp.ones((chunk, chunk), dtype=bool))
    ws = jnp.where(causal[None], w_s[:, :chunk, :chunk], 0.0)
    vc = v.reshape(B, nc, chunk, MLP_GROUPS, MLP_GW)
    mixed = jnp.einsum('gij,bcjgd->bcigd', ws, vc) + jnp.transpose(b_s[:, :chunk])[None, None, :, :, None]
    y = u * mixed.reshape(B, L, MLP_WIDTH).astype(x.dtype) * jax.nn.silu(g)
    return y @ w_out, v


def setup_inputs(seed: int = 0) -> dict:
    key = jax.random.key(seed)
    ks = jax.random.split(key, 16)
    f32 = jnp.float32
    x_prompt = jax.random.normal(ks[0], (BATCH, SEQ, D_MODEL), f32)
    x_sample = jax.random.normal(ks[1], (DEC_BATCH, DEC_SEQ, D_MODEL), f32)
    state_ret = 0.05 * jax.random.normal(ks[2], (N_RET_LAYERS, DEC_BATCH, RET_HEADS, RET_DK, RET_DV), f32)
    ln_gain = 1.0 + 0.01 * jax.random.normal(ks[3], (DEPTH, D_MODEL), f32)
    ln_bias = 0.01 * jax.random.normal(ks[4], (DEPTH, D_MODEL), f32)
    ret_scale = jnp.concatenate([
        jnp.full((2 * RET_QKW,), D_MODEL ** -0.5, f32),
        jnp.full((RET_VW,), BETA * D_MODEL ** -0.5, f32),
        jnp.full((RET_VW,), D_MODEL ** -0.5, f32)])
    w_in_ret = jax.random.normal(ks[5], (N_RET_LAYERS, D_MODEL, 2 * RET_QKW + 2 * RET_VW), f32) * ret_scale
    gn_gain_ret = 1.0 + 0.01 * jax.random.normal(ks[6], (N_RET_LAYERS, RET_VW), f32)
    w_out_ret = BETA * RET_VW ** -0.5 * jax.random.normal(ks[7], (N_RET_LAYERS, RET_VW, D_MODEL), f32)
    mlp_scale = jnp.concatenate([
        jnp.full((MLP_WIDTH,), BETA * D_MODEL ** -0.5, f32),
        jnp.full((2 * MLP_WIDTH,), D_MODEL ** -0.5, f32)])
    w_in_mlp = jax.random.normal(ks[8], (N_MLP_LAYERS, D_MODEL, 3 * MLP_WIDTH), f32) * mlp_scale
    ln_gain_mlp = 1.0 + 0.01 * jax.random.normal(ks[9], (N_MLP_LAYERS, MLP_WIDTH), f32)
    ln_bias_mlp = 0.01 * jax.random.normal(ks[10], (N_MLP_LAYERS, MLP_WIDTH), f32)
    w_spatial = MLP_CHUNK ** -0.5 * jax.random.normal(ks[11], (N_MLP_LAYERS, MLP_GROUPS, MLP_CHUNK, MLP_CHUNK), f32)
    b_spatial = 1.0 + 0.01 * jax.random.normal(ks[12], (N_MLP_LAYERS, MLP_GROUPS, MLP_CHUNK), f32)
    w_out_mlp = BETA * MLP_WIDTH ** -0.5 * jax.random.normal(ks[13], (N_MLP_LAYERS, MLP_WIDTH, D_MODEL), f32)
    return {'x_prompt': x_prompt, 'x_sample': x_sample, 'state_ret': state_ret,
            'ln_gain': ln_gain, 'ln_bias': ln_bias,
            'w_in_ret': w_in_ret, 'gn_gain_ret': gn_gain_ret, 'w_out_ret': w_out_ret,
            'w_in_mlp': w_in_mlp, 'ln_gain_mlp': ln_gain_mlp, 'ln_bias_mlp': ln_bias_mlp,
            'w_spatial': w_spatial, 'b_spatial': b_spatial, 'w_out_mlp': w_out_mlp}


def reference(x_prompt, x_sample, state_ret, ln_gain, ln_bias, w_in_ret, gn_gain_ret, w_out_ret,
              w_in_mlp, ln_gain_mlp, ln_bias_mlp, w_spatial, b_spatial, w_out_mlp):
    xp, xs = x_prompt, x_sample
    pos_p = jnp.arange(xp.shape[1], dtype=jnp.int32)
    pos_s = PAST_LEN + jnp.arange(xs.shape[1], dtype=jnp.int32)
    ret_p, ret_s, mlp_s = [], [], []
    for i in range(DEPTH):
        j = i // N_MIXERS
        if i % N_MIXERS == 0:
            s0_p = jnp.zeros((xp.shape[0], RET_HEADS, RET_DK, RET_DV), xp.dtype)
            fp, sp = _retention_mixer(xp, s0_p, pos_p, w_in_ret[j], gn_gain_ret[j], w_out_ret[j])
            fs, ss = _retention_mixer(xs, state_ret[j], pos_s, w_in_ret[j], gn_gain_ret[j], w_out_ret[j])
            ret_p.append(sp)
            ret_s.append(ss)
        else:
            fp, _ = _chunk_mlp_mixer(xp, w_in_mlp[j], ln_gain_mlp[j], ln_bias_mlp[j], w_spatial[j], b_spatial[j], w_out_mlp[j])
            fs, vs = _chunk_mlp_mixer(xs, w_in_mlp[j], ln_gain_mlp[j], ln_bias_mlp[j], w_spatial[j], b_spatial[j], w_out_mlp[j])
            mlp_s.append(vs)
        xp = _layer_norm(ALPHA * xp + fp, ln_gain[i], ln_bias[i])
        xs = _layer_norm(ALPHA * xs + fs, ln_gain[i], ln_bias[i])
    ret_state_prompt = jnp.stack(ret_p)
    ret_state_sample = jnp.stack(ret_s)
    mlp_v_sample = jnp.stack(mlp_s)
    return (xp, xs, ret_state_prompt, ret_state_sample, mlp_v_sample)
```

```python
import functools

import jax
import jax.numpy as jnp
import numpy as np
from jax import lax
from jax.experimental import pallas as pl
from jax.experimental.pallas import tpu as pltpu

F32 = jnp.float32
BF16 = jnp.bfloat16

D_MODEL = 1024
DEPTH = 2
PAST_LEN = 16384
HEADS = 4
DK = D_MODEL // HEADS
DV = 2 * DK
QKW = HEADS * DK
VW = HEADS * DV
CHUNK = 128
ROPE_BASE = 10000.0
ROPE_HALF = DK // 2
MLP_W = 2 * D_MODEL
GROUPS = 8
GW = MLP_W // GROUPS
ALPHA = (2 * DEPTH) ** 0.25
LN_EPS = 1e-5
SQRT_HALF = float(np.sqrt(0.5))

TOKEN_TILE = 512
STATE_ROWS = 2
VMEM_LIMIT = 60 * 1024 * 1024


def _dot(a, b):
    return jnp.dot(a, b, preferred_element_type=F32)


def _gelu(x):
    return 0.5 * x * (1.0 + lax.erf(x * SQRT_HALF))


def _silu(x):
    return x * jax.nn.sigmoid(x)


def _norm_rows(x):
    mu = jnp.mean(x, axis=-1, keepdims=True)
    d = x - mu
    var = jnp.mean(d * d, axis=-1, keepdims=True)
    return d * lax.rsqrt(var + LN_EPS)


def _rotate(h, cos, sin):
    x1 = h[:, :ROPE_HALF]
    x2 = h[:, ROPE_HALF:]
    return jnp.concatenate([x1 * cos - x2 * sin, x2 * cos + x1 * sin], axis=-1)


def _ret_prompt_kernel(gl_ref, x_ref, cos_ref, sin_ref, win_ref, decay_ref, cross_ref, kd_ref,
                       gn_ref, wout_ref, lng_ref, lnb_ref,
                       y_ref, st_ref,
                       q_scr, k_scr, v_scr, sg_scr, yin_scr, *, tile):
    @pl.when(pl.program_id(1) == 0)
    def _():
        st_ref[...] = jnp.zeros_like(st_ref)

    x = x_ref[...]
    xb = x.astype(BF16)
    cos = cos_ref[...]
    sin = sin_ref[...]
    for h in range(HEADS):
        qs = slice(h * DK, (h + 1) * DK)
        q_scr[:, qs] = _rotate(_dot(xb, win_ref[:, qs]), cos, sin).astype(BF16)
        ks = slice(QKW + h * DK, QKW + (h + 1) * DK)
        k_scr[:, qs] = _rotate(_dot(xb, win_ref[:, ks]), cos, sin) * (DK ** -0.5)
    for h in range(HEADS):
        vs = slice(h * DV, (h + 1) * DV)
        v_scr[:, vs] = _dot(xb, win_ref[:, 2 * QKW + h * DV:2 * QKW + (h + 1) * DV]).astype(BF16)
        sg_scr[:, vs] = _silu(_dot(xb, win_ref[:, 2 * QKW + VW + h * DV:2 * QKW + VW + (h + 1) * DV]))

    for c in range(tile // CHUNK):
        rows = slice(c * CHUNK, (c + 1) * CHUNK)
        for h in range(HEADS):
            qs = slice(h * DK, (h + 1) * DK)
            vs = slice(h * DV, (h + 1) * DV)
            qc = q_scr[rows, qs]
            kf = k_scr[rows, qs]
            vc = v_scr[rows, vs]
            s = lax.dot_general(qc, kf.astype(BF16), (((1,), (1,)), ((), ())),
                                preferred_element_type=F32)
            s = s * decay_ref[h]
            state = st_ref[h]
            o = _dot(s.astype(BF16), vc) + _dot(qc, state.astype(BF16)) * cross_ref[h]
            kdec = (kf * kd_ref[h]).astype(BF16)
            st_ref[h] = gl_ref[h] * state + lax.dot_general(
                kdec, vc, (((0,), (0,)), ((), ())), preferred_element_type=F32)
            on = _norm_rows(o) * gn_ref[:, vs]
            yin_scr[rows, vs] = (sg_scr[rows, vs] * on).astype(BF16)

    z = ALPHA * x + _dot(yin_scr[...], wout_ref[...])
    y_ref[...] = _norm_rows(z) * lng_ref[...] + lnb_ref[...]


def _ret_prompt(x, cos, sin, w_in, decay, cross, kd, gl, gn_gain, w_out, ln_g, ln_b):
    batch, seq, _ = x.shape
    tile = TOKEN_TILE
    const = lambda *shape: pl.BlockSpec(shape, lambda b, t: (0,) * len(shape),
                                        pipeline_mode=pl.Buffered(1))
    return pl.pallas_call(
        functools.partial(_ret_prompt_kernel, tile=tile),
        grid=(batch, seq // tile),
        in_specs=[
            pl.BlockSpec(memory_space=pltpu.SMEM),
            pl.BlockSpec((None, tile, D_MODEL), lambda b, t: (b, t, 0)),
            pl.BlockSpec((tile, ROPE_HALF), lambda b, t: (t, 0)),
            pl.BlockSpec((tile, ROPE_HALF), lambda b, t: (t, 0)),
            const(D_MODEL, 2 * QKW + 2 * VW),
            const(HEADS, CHUNK, CHUNK),
            const(HEADS, CHUNK, DV),
            const(HEADS, CHUNK, DK),
            const(1, VW),
            const(VW, D_MODEL),
            const(1, D_MODEL),
            const(1, D_MODEL),
        ],
        out_specs=[
            pl.BlockSpec((None, tile, D_MODEL), lambda b, t: (b, t, 0)),
            pl.BlockSpec((None, HEADS, DK, DV), lambda b, t: (b, 0, 0, 0)),
        ],
        out_shape=[
            jax.ShapeDtypeStruct((batch, seq, D_MODEL), F32),
            jax.ShapeDtypeStruct((batch, HEADS, DK, DV), F32),
        ],
        scratch_shapes=[
            pltpu.VMEM((tile, QKW), BF16),
            pltpu.VMEM((tile, QKW), F32),
            pltpu.VMEM((tile, VW), BF16),
            pltpu.VMEM((tile, VW), F32),
            pltpu.VMEM((tile, VW), BF16),
        ],
        compiler_params=pltpu.CompilerParams(
            dimension_semantics=("arbitrary", "arbitrary"),
            vmem_limit_bytes=VMEM_LIMIT),
        name="ret_prompt",
    )(gl, x, cos, sin, w_in, decay, cross, kd, gn_gain, w_out, ln_g, ln_b)


def _mlp_prompt_kernel(x_ref, win_ref, lngm_ref, lnbm_ref, ws_ref, bsp_ref, wout_ref, lng_ref, lnb_ref,
                       y_ref, p_scr, gv_scr, yin_scr, *, tile):
    x = x_ref[...]
    xb = x.astype(BF16)
    for j in range(MLP_W // DV):
        cs = slice(j * DV, (j + 1) * DV)
        u = _dot(xb, win_ref[:, j * DV:(j + 1) * DV])
        g = _dot(xb, win_ref[:, 2 * MLP_W + j * DV:2 * MLP_W + (j + 1) * DV])
        p_scr[:, cs] = _gelu(u) * _silu(g)
        gv_scr[:, cs] = _gelu(_dot(xb, win_ref[:, MLP_W + j * DV:MLP_W + (j + 1) * DV]))

    row = lax.broadcasted_iota(jnp.int32, (CHUNK, CHUNK), 0)
    col = lax.broadcasted_iota(jnp.int32, (CHUNK, CHUNK), 1)
    ws = [jnp.where(row >= col, ws_ref[g], 0.0).astype(BF16) for g in range(GROUPS)]
    for c in range(tile // CHUNK):
        rows = slice(c * CHUNK, (c + 1) * CHUNK)
        vn = (_norm_rows(gv_scr[rows, :]) * lngm_ref[...] + lnbm_ref[...]).astype(BF16)
        for g in range(GROUPS):
            gs = slice(g * GW, (g + 1) * GW)
            mixed = _dot(ws[g], vn[:, gs]) + bsp_ref[:, gs]
            yin_scr[rows, gs] = (p_scr[rows, gs] * mixed).astype(BF16)

    z = ALPHA * x + _dot(yin_scr[...], wout_ref[...])
    y_ref[...] = _norm_rows(z) * lng_ref[...] + lnb_ref[...]


def _mlp_prompt(x, w_in, lng_m, lnb_m, w_s, bsp, w_out, ln_g, ln_b):
    n = x.shape[0]
    tile = TOKEN_TILE
    const = lambda *shape: pl.BlockSpec(shape, lambda i: (0,) * len(shape),
                                        pipeline_mode=pl.Buffered(1))
    return pl.pallas_call(
        functools.partial(_mlp_prompt_kernel, tile=tile),
        grid=(n // tile,),
        in_specs=[
            pl.BlockSpec((tile, D_MODEL), lambda i: (i, 0)),
            const(D_MODEL, 3 * MLP_W),
            const(1, MLP_W),
            const(1, MLP_W),
            const(GROUPS, CHUNK, CHUNK),
            const(CHUNK, MLP_W),
            const(MLP_W, D_MODEL),
            const(1, D_MODEL),
            const(1, D_MODEL),
        ],
        out_specs=pl.BlockSpec((tile, D_MODEL), lambda i: (i, 0)),
        out_shape=jax.ShapeDtypeStruct((n, D_MODEL), F32),
        scratch_shapes=[
            pltpu.VMEM((tile, MLP_W), F32),
            pltpu.VMEM((tile, MLP_W), F32),
            pltpu.VMEM((tile, MLP_W), BF16),
        ],
        compiler_params=pltpu.CompilerParams(
            dimension_semantics=("arbitrary",),
            vmem_limit_bytes=VMEM_LIMIT),
        name="mlp_prompt",
    )(x, w_in, lng_m, lnb_m, w_s, bsp, w_out, ln_g, ln_b)


def _ret_sample_in_kernel(x_ref, cos_ref, sin_ref, win_ref,
                          q_ref, kt_ref, v_ref, sg_ref, oin_ref):
    xb = x_ref[...].astype(BF16)
    cos = cos_ref[...]
    sin = sin_ref[...]
    for h in range(HEADS):
        qs = slice(h * DK, (h + 1) * DK)
        vs = slice(h * DV, (h + 1) * DV)
        q = _rotate(_dot(xb, win_ref[:, qs]), cos, sin)
        k = _rotate(_dot(xb, win_ref[:, QKW + h * DK:QKW + (h + 1) * DK]), cos, sin) * (DK ** -0.5)
        v = _dot(xb, win_ref[:, 2 * QKW + h * DV:2 * QKW + (h + 1) * DV]).astype(BF16)
        g = _dot(xb, win_ref[:, 2 * QKW + VW + h * DV:2 * QKW + VW + (h + 1) * DV])
        q_ref[:, qs] = q
        kt_ref[h] = k.T
        v_ref[:, vs] = v
        sg_ref[:, vs] = _silu(g)
        qk = jnp.sum(q.astype(BF16).astype(F32) * k.astype(BF16).astype(F32), axis=-1, keepdims=True)
        oin_ref[:, vs] = qk.astype(BF16).astype(F32) * v.astype(F32)


def _ret_sample_in(x, cos, sin, w_in):
    n = x.shape[0]
    return pl.pallas_call(
        _ret_sample_in_kernel,
        out_shape=[
            jax.ShapeDtypeStruct((n, QKW), F32),
            jax.ShapeDtypeStruct((HEADS, DK, n), F32),
            jax.ShapeDtypeStruct((n, VW), BF16),
            jax.ShapeDtypeStruct((n, VW), F32),
            jax.ShapeDtypeStruct((n, VW), F32),
        ],
        compiler_params=pltpu.CompilerParams(vmem_limit_bytes=VMEM_LIMIT),
        name="ret_sample_in",
    )(x, cos, sin, w_in)


def _ret_sample_state_kernel(gam_ref, s0_ref, q_ref, kt_ref, v_ref, s1_ref, oc_ref, *, rows):
    i = pl.program_id(0)

    @pl.when(i == 0)
    def _():
        oc_ref[...] = jnp.zeros_like(oc_ref)

    n = kt_ref.shape[-1]
    lane = lax.broadcasted_iota(jnp.int32, (DK, n), 1)
    sub = lax.broadcasted_iota(jnp.int32, (16, DK), 0)
    for j in range(rows):
        b = i * rows + j
        r16 = pl.multiple_of((b // 16) * 16, 16)
        for h in range(HEADS):
            qs = slice(h * DK, (h + 1) * DK)
            vs = slice(h * DV, (h + 1) * DV)
            s0 = s0_ref[j, h]
            qm = jnp.where(sub == b - r16, q_ref[pl.ds(r16, 16), qs], 0.0).astype(BF16)
            oc_ref[pl.ds(r16, 16), vs] += _dot(qm, s0.astype(BF16))
            km = jnp.where(lane == b, kt_ref[h], 0.0).astype(BF16)
            s1_ref[j, h] = gam_ref[h] * s0 + _dot(km, v_ref[:, vs])


def _ret_sample_state(gamma, s0, q, kt, v):
    n = s0.shape[0]
    rows = STATE_ROWS
    return pl.pallas_call(
        functools.partial(_ret_sample_state_kernel, rows=rows),
        grid=(n // rows,),
        in_specs=[
            pl.BlockSpec(memory_space=pltpu.SMEM),
            pl.BlockSpec((rows, HEADS, DK, DV), lambda i: (i, 0, 0, 0)),
            pl.BlockSpec((n, QKW), lambda i: (0, 0)),
            pl.BlockSpec((HEADS, DK, n), lambda i: (0, 0, 0)),
            pl.BlockSpec((n, VW), lambda i: (0, 0)),
        ],
        out_specs=[
            pl.BlockSpec((rows, HEADS, DK, DV), lambda i: (i, 0, 0, 0)),
            pl.BlockSpec((n, VW), lambda i: (0, 0)),
        ],
        out_shape=[
            jax.ShapeDtypeStruct(s0.shape, F32),
            jax.ShapeDtypeStruct((n, VW), F32),
        ],
        compiler_params=pltpu.CompilerParams(
            dimension_semantics=("arbitrary",),
            vmem_limit_bytes=VMEM_LIMIT),
        name="ret_sample_state",
    )(gamma, s0, q, kt, v)


def _sample_finish_kernel(gam_ref, x_ref, oin_ref, oc_ref, sg_ref, gn_ref, wout0_ref, lng0_ref, lnb0_ref,
                          win_ref, lngm_ref, lnbm_ref, wsd_ref, bsd_ref, wout1_ref, lng1_ref, lnb1_ref,
                          y_ref, vout_ref, yin_scr):
    x = x_ref[...]
    for h in range(HEADS):
        vs = slice(h * DV, (h + 1) * DV)
        o = oin_ref[:, vs] + oc_ref[:, vs] * gam_ref[h]
        yin_scr[:, vs] = (sg_ref[:, vs] * (_norm_rows(o) * gn_ref[:, vs])).astype(BF16)
    z = ALPHA * x + _dot(yin_scr[...], wout0_ref[...])
    x1 = _norm_rows(z) * lng0_ref[...] + lnb0_ref[...]

    xb = x1.astype(BF16)
    u = _gelu(_dot(xb, win_ref[:, :MLP_W]))
    v = _gelu(_dot(xb, win_ref[:, MLP_W:2 * MLP_W]))
    g = _dot(xb, win_ref[:, 2 * MLP_W:])
    vn = _norm_rows(v) * lngm_ref[...] + lnbm_ref[...]
    vout_ref[...] = vn
    mixed = wsd_ref[...].astype(BF16).astype(F32) * vn.astype(BF16).astype(F32) + bsd_ref[...]
    yin = (u * mixed * _silu(g)).astype(BF16)
    z1 = ALPHA * x1 + _dot(yin, wout1_ref[...])
    y_ref[...] = _norm_rows(z1) * lng1_ref[...] + lnb1_ref[...]


def _sample_finish(gamma, x, oin, oc, sg, gn_gain, w_out0, lng0, lnb0,
                   w_in1, lng_m, lnb_m, wsd, bsd, w_out1, lng1, lnb1):
    n = x.shape[0]
    vmem = pl.BlockSpec(memory_space=pltpu.VMEM)
    return pl.pallas_call(
        _sample_finish_kernel,
        in_specs=[pl.BlockSpec(memory_space=pltpu.SMEM)] + [vmem] * 16,
        out_shape=[
            jax.ShapeDtypeStruct((n, D_MODEL), F32),
            jax.ShapeDtypeStruct((n, MLP_W), F32),
        ],
        scratch_shapes=[pltpu.VMEM((n, VW), BF16)],
        compiler_params=pltpu.CompilerParams(vmem_limit_bytes=VMEM_LIMIT),
        name="sample_finish",
    )(gamma, x, oin, oc, sg, gn_gain, w_out0, lng0, lnb0,
      w_in1, lng_m, lnb_m, wsd, bsd, w_out1, lng1, lnb1)


def _rope_tables(pos):
    inv = ROPE_BASE ** (-jnp.arange(ROPE_HALF, dtype=F32) / ROPE_HALF)
    ang = pos.astype(F32)[:, None] * inv[None, :]
    return jnp.cos(ang), jnp.sin(ang)


def _decay_tables():
    lg = jnp.log1p(-jnp.exp2(-5.0 - jnp.arange(HEADS, dtype=F32)))
    idx = jnp.arange(CHUNK, dtype=F32)
    diff = idx[:, None] - idx[None, :]
    decay = jnp.where(diff >= 0, jnp.exp(jnp.maximum(diff, 0.0)[None] * lg[:, None, None]), 0.0)
    cross = jnp.exp((idx[:, None] + 1.0) * lg[None, :])
    kd = jnp.exp((CHUNK - 1.0 - idx)[:, None] * lg[None, :])
    cross = jnp.broadcast_to(cross.T[:, :, None], (HEADS, CHUNK, DV))
    kd = jnp.broadcast_to(kd.T[:, :, None], (HEADS, CHUNK, DK))
    return decay, cross, kd, jnp.exp(CHUNK * lg), jnp.exp(lg)


def kernel(x_prompt, x_sample, state_ret, ln_gain, ln_bias, w_in_ret, gn_gain_ret, w_out_ret,
           w_in_mlp, ln_gain_mlp, ln_bias_mlp, w_spatial, b_spatial, w_out_mlp):
    batch, seq, _ = x_prompt.shape
    n_dec = x_sample.shape[0]
    assert x_sample.shape[1] == 1 and seq % TOKEN_TILE == 0

    w_in0 = w_in_ret[0].astype(BF16)
    w_out0 = w_out_ret[0].astype(BF16)
    w_in1 = w_in_mlp[0].astype(BF16)
    w_out1 = w_out_mlp[0].astype(BF16)
    lng0, lnb0 = ln_gain[0][None], ln_bias[0][None]
    lng1, lnb1 = ln_gain[1][None], ln_bias[1][None]
    gn_gain = gn_gain_ret[0][None]
    lng_m, lnb_m = ln_gain_mlp[0][None], ln_bias_mlp[0][None]

    cos_p, sin_p = _rope_tables(jnp.arange(seq, dtype=jnp.int32))
    cos_s, sin_s = _rope_tables(PAST_LEN + jnp.arange(1, dtype=jnp.int32))
    decay, cross, kd, gamma_chunk, gamma = _decay_tables()

    x1, ret_state_prompt = _ret_prompt(x_prompt, cos_p, sin_p, w_in0, decay, cross, kd, gamma_chunk,
                                       gn_gain, w_out0, lng0, lnb0)
    bsp = jnp.repeat(jnp.transpose(b_spatial[0]), GW, axis=1)
    y_prompt = _mlp_prompt(x1.reshape(batch * seq, D_MODEL), w_in1, lng_m, lnb_m, w_spatial[0], bsp,
                           w_out1, lng1, lnb1).reshape(batch, seq, D_MODEL)

    xs = x_sample.reshape(n_dec, D_MODEL)
    q, kt, v, sg, oin = _ret_sample_in(xs, cos_s, sin_s, w_in0)
    ret_state_sample, oc = _ret_sample_state(gamma, state_ret[0], q, kt, v)
    wsd = jnp.repeat(w_spatial[0, :, 0, 0], GW)[None]
    bsd = jnp.repeat(b_spatial[0, :, 0], GW)[None]
    y_sample, mlp_v = _sample_finish(gamma, xs, oin, oc, sg, gn_gain, w_out0, lng0, lnb0,
                                     w_in1, lng_m, lnb_m, wsd, bsd, w_out1, lng1, lnb1)

    return (y_prompt,
            y_sample.reshape(n_dec, 1, D_MODEL),
            ret_state_prompt[None],
            ret_state_sample[None],
            mlp_v.reshape(1, n_dec, 1, MLP_W))
```

```python
import functools

import jax
import jax.numpy as jnp
import numpy as np
from jax import lax
from jax.experimental import pallas as pl
from jax.experimental.pallas import tpu as pltpu

F32 = jnp.float32
BF16 = jnp.bfloat16

D_MODEL = 1024
DEPTH = 2
PAST_LEN = 16384
HEADS = 4
DK = D_MODEL // HEADS
DV = 2 * DK
QKW = HEADS * DK
VW = HEADS * DV
CHUNK = 128
ROPE_BASE = 10000.0
ROPE_HALF = DK // 2
MLP_W = 2 * D_MODEL
GROUPS = 8
GW = MLP_W // GROUPS
ALPHA = (2 * DEPTH) ** 0.25
LN_EPS = 1e-5
SQRT_HALF = float(np.sqrt(0.5))

RET_TILE = 512
MLP_TILE = 256
VMEM_LIMIT = 60 * 1024 * 1024


def _dot(a, b):
    return jnp.dot(a, b, preferred_element_type=F32)


def _gelu(x):
    return 0.5 * x * (1.0 + lax.erf(x * SQRT_HALF))


def _silu(x):
    return x * jax.nn.sigmoid(x)


def _norm_rows(x):
    mu = jnp.mean(x, axis=-1, keepdims=True)
    d = x - mu
    var = jnp.mean(d * d, axis=-1, keepdims=True)
    return d * lax.rsqrt(var + LN_EPS)


def _rotate(h, cos, sin):
    x1 = h[:, :ROPE_HALF]
    x2 = h[:, ROPE_HALF:]
    return jnp.concatenate([x1 * cos - x2 * sin, x2 * cos + x1 * sin], axis=-1)


def _ret_prompt_kernel(gl_ref, x_ref, cos_ref, sin_ref, win_ref, decay_ref, cross_ref, kd_ref,
                       gn_ref, wout_ref, lng_ref, lnb_ref,
                       y_ref, st_ref,
                       q_scr, k_scr, v_scr, sg_scr, yin_scr, *, tile):
    @pl.when(pl.program_id(1) == 0)
    def _():
        st_ref[...] = jnp.zeros_like(st_ref)

    x = x_ref[...]
    xb = x.astype(BF16)
    cos = cos_ref[...]
    sin = sin_ref[...]
    for h in range(HEADS):
        qs = slice(h * DK, (h + 1) * DK)
        q_scr[:, qs] = _rotate(_dot(xb, win_ref[:, qs]), cos, sin).astype(BF16)
        ks = slice(QKW + h * DK, QKW + (h + 1) * DK)
        k_scr[:, qs] = _rotate(_dot(xb, win_ref[:, ks]), cos, sin) * (DK ** -0.5)
    for h in range(HEADS):
        vs = slice(h * DV, (h + 1) * DV)
        v_scr[:, vs] = _dot(xb, win_ref[:, 2 * QKW + h * DV:2 * QKW + (h + 1) * DV]).astype(BF16)
        sg_scr[:, vs] = _silu(_dot(xb, win_ref[:, 2 * QKW + VW + h * DV:2 * QKW + VW + (h + 1) * DV]))

    for c in range(tile // CHUNK):
        rows = slice(c * CHUNK, (c + 1) * CHUNK)
        for h in range(HEADS):
            qs = slice(h * DK, (h + 1) * DK)
            vs = slice(h * DV, (h + 1) * DV)
            qc = q_scr[rows, qs]
            kf = k_scr[rows, qs]
            vc = v_scr[rows, vs]
            s = lax.dot_general(qc, kf.astype(BF16), (((1,), (1,)), ((), ())),
                                preferred_element_type=F32)
            s = s * decay_ref[h]
            state = st_ref[h]
            o = _dot(s.astype(BF16), vc) + _dot(qc, state.astype(BF16)) * cross_ref[h]
            kdec = (kf * kd_ref[h]).astype(BF16)
            st_ref[h] = gl_ref[h] * state + lax.dot_general(
                kdec, vc, (((0,), (0,)), ((), ())), preferred_element_type=F32)
            on = _norm_rows(o) * gn_ref[:, vs]
            yin_scr[rows, vs] = (sg_scr[rows, vs] * on).astype(BF16)

    z = ALPHA * x + _dot(yin_scr[...], wout_ref[...])
    y_ref[...] = _norm_rows(z) * lng_ref[...] + lnb_ref[...]


def _ret_prompt(x, cos, sin, w_in, decay, cross, kd, gl, gn_gain, w_out, ln_g, ln_b):
    batch, seq, _ = x.shape
    tile = RET_TILE
    const = lambda *shape: pl.BlockSpec(shape, lambda b, t: (0,) * len(shape),
                                        pipeline_mode=pl.Buffered(1))
    return pl.pallas_call(
        functools.partial(_ret_prompt_kernel, tile=tile),
        grid=(batch, seq // tile),
        in_specs=[
            pl.BlockSpec(memory_space=pltpu.SMEM),
            pl.BlockSpec((None, tile, D_MODEL), lambda b, t: (b, t, 0)),
            pl.BlockSpec((tile, ROPE_HALF), lambda b, t: (t, 0)),
            pl.BlockSpec((tile, ROPE_HALF), lambda b, t: (t, 0)),
            const(D_MODEL, 2 * QKW + 2 * VW),
            const(HEADS, CHUNK, CHUNK),
            const(HEADS, CHUNK, DV),
            const(HEADS, CHUNK, DK),
            const(1, VW),
            const(VW, D_MODEL),
            const(1, D_MODEL),
            const(1, D_MODEL),
        ],
        out_specs=[
            pl.BlockSpec((None, tile, D_MODEL), lambda b, t: (b, t, 0)),
            pl.BlockSpec((None, HEADS, DK, DV), lambda b, t: (b, 0, 0, 0)),
        ],
        out_shape=[
            jax.ShapeDtypeStruct((batch, seq, D_MODEL), F32),
            jax.ShapeDtypeStruct((batch, HEADS, DK, DV), F32),
        ],
        scratch_shapes=[
            pltpu.VMEM((tile, QKW), BF16),
            pltpu.VMEM((tile, QKW), F32),
            pltpu.VMEM((tile, VW), BF16),
            pltpu.VMEM((tile, VW), F32),
            pltpu.VMEM((tile, VW), BF16),
        ],
        compiler_params=pltpu.CompilerParams(
            dimension_semantics=("arbitrary", "arbitrary"),
            vmem_limit_bytes=VMEM_LIMIT),
        name="ret_prompt",
    )(gl, x, cos, sin, w_in, decay, cross, kd, gn_gain, w_out, ln_g, ln_b)


def _state_update(b, s0_ref, s1_ref, gam_ref, q_ref, kt_ref, v_ref, oc_ref):
    n = kt_ref.shape[-1]
    lane = lax.broadcasted_iota(jnp.int32, (DK, n), 1)
    sub = lax.broadcasted_iota(jnp.int32, (16, DK), 0)
    r16 = pl.multiple_of(lax.shift_right_logical(b, 4) * 16, 16)
    for h in range(HEADS):
        qs = slice(h * DK, (h + 1) * DK)
        vs = slice(h * DV, (h + 1) * DV)
        s0 = s0_ref[h]
        qm = jnp.where(sub == b - r16, q_ref[pl.ds(r16, 16), qs], 0.0).astype(BF16)
        oc_ref[pl.ds(r16, 16), vs] += _dot(qm, s0.astype(BF16))
        km = jnp.where(lane == b, kt_ref[h], 0.0).astype(BF16)
        s1_ref[h] = gam_ref[h] * s0 + _dot(km, v_ref[:, vs])


def _mlp_prompt_kernel(gam_ref, x_ref, win_ref, lngm_ref, lnbm_ref, ws_ref, bsp_ref, wout_ref, lng_ref, lnb_ref,
                       s0_ref, q_ref, kt_ref, v_ref,
                       y_ref, s1_ref, oc_ref,
                       p_scr, gv_scr, yin_scr, *, tile, seqs):
    i = pl.program_id(0)

    @pl.when(i == 0)
    def _():
        oc_ref[...] = jnp.zeros_like(oc_ref)

    x = x_ref[...]
    xb = x.astype(BF16)
    for j in range(MLP_W // DV):
        cs = slice(j * DV, (j + 1) * DV)
        u = _dot(xb, win_ref[:, j * DV:(j + 1) * DV])
        g = _dot(xb, win_ref[:, 2 * MLP_W + j * DV:2 * MLP_W + (j + 1) * DV])
        p_scr[:, cs] = _gelu(u) * _silu(g)
        gv_scr[:, cs] = _gelu(_dot(xb, win_ref[:, MLP_W + j * DV:MLP_W + (j + 1) * DV]))

    for j in range(seqs):
        _state_update(i * seqs + j, s0_ref.at[j], s1_ref.at[j], gam_ref, q_ref, kt_ref, v_ref, oc_ref)

    row = lax.broadcasted_iota(jnp.int32, (CHUNK, CHUNK), 0)
    col = lax.broadcasted_iota(jnp.int32, (CHUNK, CHUNK), 1)
    ws = [jnp.where(row >= col, ws_ref[g], 0.0).astype(BF16) for g in range(GROUPS)]
    for c in range(tile // CHUNK):
        rows = slice(c * CHUNK, (c + 1) * CHUNK)
        vn = (_norm_rows(gv_scr[rows, :]) * lngm_ref[...] + lnbm_ref[...]).astype(BF16)
        for g in range(GROUPS):
            gs = slice(g * GW, (g + 1) * GW)
            mixed = _dot(ws[g], vn[:, gs]) + bsp_ref[:, gs]
            yin_scr[rows, gs] = (p_scr[rows, gs] * mixed).astype(BF16)

    z = ALPHA * x + _dot(yin_scr[...], wout_ref[...])
    y_ref[...] = _norm_rows(z) * lng_ref[...] + lnb_ref[...]


def _mlp_prompt(x, w_in, lng_m, lnb_m, w_s, bsp, w_out, ln_g, ln_b, gamma, s0, q, kt, v):
    n = x.shape[0]
    tile = MLP_TILE
    steps = n // tile
    n_dec = s0.shape[0]
    seqs = n_dec // steps
    assert seqs * steps == n_dec
    const = lambda *shape: pl.BlockSpec(shape, lambda i: (0,) * len(shape),
                                        pipeline_mode=pl.Buffered(1))
    return pl.pallas_call(
        functools.partial(_mlp_prompt_kernel, tile=tile, seqs=seqs),
        grid=(steps,),
        in_specs=[
            pl.BlockSpec(memory_space=pltpu.SMEM),
            pl.BlockSpec((tile, D_MODEL), lambda i: (i, 0)),
            const(D_MODEL, 3 * MLP_W),
            const(1, MLP_W),
            const(1, MLP_W),
            const(GROUPS, CHUNK, CHUNK),
            const(CHUNK, MLP_W),
            const(MLP_W, D_MODEL),
            const(1, D_MODEL),
            const(1, D_MODEL),
            pl.BlockSpec((seqs, HEADS, DK, DV), lambda i: (i, 0, 0, 0)),
            const(n_dec, QKW),
            const(HEADS, DK, n_dec),
            const(n_dec, VW),
        ],
        out_specs=[
            pl.BlockSpec((tile, D_MODEL), lambda i: (i, 0)),
            pl.BlockSpec((seqs, HEADS, DK, DV), lambda i: (i, 0, 0, 0)),
            pl.BlockSpec((n_dec, VW), lambda i: (0, 0)),
        ],
        out_shape=[
            jax.ShapeDtypeStruct((n, D_MODEL), F32),
            jax.ShapeDtypeStruct(s0.shape, F32),
            jax.ShapeDtypeStruct((n_dec, VW), F32),
        ],
        scratch_shapes=[
            pltpu.VMEM((tile, MLP_W), F32),
            pltpu.VMEM((tile, MLP_W), F32),
            pltpu.VMEM((tile, MLP_W), BF16),
        ],
        compiler_params=pltpu.CompilerParams(
            dimension_semantics=("arbitrary",),
            vmem_limit_bytes=VMEM_LIMIT),
        name="mlp_prompt",
    )(gamma, x, w_in, lng_m, lnb_m, w_s, bsp, w_out, ln_g, ln_b, s0, q, kt, v)


def _ret_sample_in_kernel(x_ref, cos_ref, sin_ref, win_ref,
                          q_ref, kt_ref, v_ref, sg_ref, oin_ref):
    xb = x_ref[...].astype(BF16)
    cos = cos_ref[...]
    sin = sin_ref[...]
    for h in range(HEADS):
        qs = slice(h * DK, (h + 1) * DK)
        vs = slice(h * DV, (h + 1) * DV)
        q = _rotate(_dot(xb, win_ref[:, qs]), cos, sin)
        k = _rotate(_dot(xb, win_ref[:, QKW + h * DK:QKW + (h + 1) * DK]), cos, sin) * (DK ** -0.5)
        v = _dot(xb, win_ref[:, 2 * QKW + h * DV:2 * QKW + (h + 1) * DV]).astype(BF16)
        g = _dot(xb, win_ref[:, 2 * QKW + VW + h * DV:2 * QKW + VW + (h + 1) * DV])
        q_ref[:, qs] = q
        kt_ref[h] = k.T
        v_ref[:, vs] = v
        sg_ref[:, vs] = _silu(g)
        qk = jnp.sum(q.astype(BF16).astype(F32) * k.astype(BF16).astype(F32), axis=-1, keepdims=True)
        oin_ref[:, vs] = qk.astype(BF16).astype(F32) * v.astype(F32)


def _ret_sample_in(x, cos, sin, w_in):
    n = x.shape[0]
    return pl.pallas_call(
        _ret_sample_in_kernel,
        out_shape=[
            jax.ShapeDtypeStruct((n, QKW), F32),
            jax.ShapeDtypeStruct((HEADS, DK, n), F32),
            jax.ShapeDtypeStruct((n, VW), BF16),
            jax.ShapeDtypeStruct((n, VW), F32),
            jax.ShapeDtypeStruct((n, VW), F32),
        ],
        compiler_params=pltpu.CompilerParams(vmem_limit_bytes=VMEM_LIMIT),
        name="ret_sample_in",
    )(x, cos, sin, w_in)


def _sample_finish_kernel(gam_ref, x_ref, oin_ref, oc_ref, sg_ref, gn_ref, wout0_ref, lng0_ref, lnb0_ref,
                          win_ref, lngm_ref, lnbm_ref, wsd_ref, bsd_ref, wout1_ref, lng1_ref, lnb1_ref,
                          y_ref, vout_ref, yin_scr):
    x = x_ref[...]
    for h in range(HEADS):
        vs = slice(h * DV, (h + 1) * DV)
        o = oin_ref[:, vs] + oc_ref[:, vs] * gam_ref[h]
        yin_scr[:, vs] = (sg_ref[:, vs] * (_norm_rows(o) * gn_ref[:, vs])).astype(BF16)
    z = ALPHA * x + _dot(yin_scr[...], wout0_ref[...])
    x1 = _norm_rows(z) * lng0_ref[...] + lnb0_ref[...]

    xb = x1.astype(BF16)
    u = _gelu(_dot(xb, win_ref[:, :MLP_W]))
    v = _gelu(_dot(xb, win_ref[:, MLP_W:2 * MLP_W]))
    g = _dot(xb, win_ref[:, 2 * MLP_W:])
    vn = _norm_rows(v) * lngm_ref[...] + lnbm_ref[...]
    vout_ref[...] = vn
    mixed = wsd_ref[...].astype(BF16).astype(F32) * vn.astype(BF16).astype(F32) + bsd_ref[...]
    yin = (u * mixed * _silu(g)).astype(BF16)
    z1 = ALPHA * x1 + _dot(yin, wout1_ref[...])
    y_ref[...] = _norm_rows(z1) * lng1_ref[...] + lnb1_ref[...]


def _sample_finish(gamma, x, oin, oc, sg, gn_gain, w_out0, lng0, lnb0,
                   w_in1, lng_m, lnb_m, wsd, bsd, w_out1, lng1, lnb1):
    n = x.shape[0]
    vmem = pl.BlockSpec(memory_space=pltpu.VMEM)
    return pl.pallas_call(
        _sample_finish_kernel,
        in_specs=[pl.BlockSpec(memory_space=pltpu.SMEM)] + [vmem] * 16,
        out_shape=[
            jax.ShapeDtypeStruct((n, D_MODEL), F32),
            jax.ShapeDtypeStruct((n, MLP_W), F32),
        ],
        scratch_shapes=[pltpu.VMEM((n, VW), BF16)],
        compiler_params=pltpu.CompilerParams(vmem_limit_bytes=VMEM_LIMIT),
        name="sample_finish",
    )(gamma, x, oin, oc, sg, gn_gain, w_out0, lng0, lnb0,
      w_in1, lng_m, lnb_m, wsd, bsd, w_out1, lng1, lnb1)


def _rope_tables(pos):
    inv = ROPE_BASE ** (-jnp.arange(ROPE_HALF, dtype=F32) / ROPE_HALF)
    ang = pos.astype(F32)[:, None] * inv[None, :]
    return jnp.cos(ang), jnp.sin(ang)


def _decay_tables():
    lg = jnp.log1p(-jnp.exp2(-5.0 - jnp.arange(HEADS, dtype=F32)))
    idx = jnp.arange(CHUNK, dtype=F32)
    diff = idx[:, None] - idx[None, :]
    decay = jnp.where(diff >= 0, jnp.exp(jnp.maximum(diff, 0.0)[None] * lg[:, None, None]), 0.0)
    cross = jnp.exp((idx[:, None] + 1.0) * lg[None, :])
    kd = jnp.exp((CHUNK - 1.0 - idx)[:, None] * lg[None, :])
    cross = jnp.broadcast_to(cross.T[:, :, None], (HEADS, CHUNK, DV))
    kd = jnp.broadcast_to(kd.T[:, :, None], (HEADS, CHUNK, DK))
    return decay, cross, kd, jnp.exp(CHUNK * lg), jnp.exp(lg)


def kernel(x_prompt, x_sample, state_ret, ln_gain, ln_bias, w_in_ret, gn_gain_ret, w_out_ret,
           w_in_mlp, ln_gain_mlp, ln_bias_mlp, w_spatial, b_spatial, w_out_mlp):
    batch, seq, _ = x_prompt.shape
    n_dec = x_sample.shape[0]
    assert x_sample.shape[1] == 1 and seq % RET_TILE == 0

    w_in0 = w_in_ret[0].astype(BF16)
    w_out0 = w_out_ret[0].astype(BF16)
    w_in1 = w_in_mlp[0].astype(BF16)
    w_out1 = w_out_mlp[0].astype(BF16)
    lng0, lnb0 = ln_gain[0][None], ln_bias[0][None]
    lng1, lnb1 = ln_gain[1][None], ln_bias[1][None]
    gn_gain = gn_gain_ret[0][None]
    lng_m, lnb_m = ln_gain_mlp[0][None], ln_bias_mlp[0][None]

    cos_p, sin_p = _rope_tables(jnp.arange(seq, dtype=jnp.int32))
    cos_s, sin_s = _rope_tables(PAST_LEN + jnp.arange(1, dtype=jnp.int32))
    decay, cross, kd, gamma_chunk, gamma = _decay_tables()

    xs = x_sample.reshape(n_dec, D_MODEL)
    q, kt, v, sg, oin = _ret_sample_in(xs, cos_s, sin_s, w_in0)

    x1, ret_state_prompt = _ret_prompt(x_prompt, cos_p, sin_p, w_in0, decay, cross, kd, gamma_chunk,
                                       gn_gain, w_out0, lng0, lnb0)
    bsp = jnp.repeat(jnp.transpose(b_spatial[0]), GW, axis=1)
    y_prompt, ret_state_sample, oc = _mlp_prompt(
        x1.reshape(batch * seq, D_MODEL), w_in1, lng_m, lnb_m, w_spatial[0], bsp, w_out1, lng1, lnb1,
        gamma, state_ret[0], q, kt, v)
    y_prompt = y_prompt.reshape(batch, seq, D_MODEL)

    wsd = jnp.repeat(w_spatial[0, :, 0, 0], GW)[None]
    bsd = jnp.repeat(b_spatial[0, :, 0], GW)[None]
    y_sample, mlp_v = _sample_finish(gamma, xs, oin, oc, sg, gn_gain, w_out0, lng0, lnb0,
                                     w_in1, lng_m, lnb_m, wsd, bsd, w_out1, lng1, lnb1)

    return (y_prompt,
            y_sample.reshape(n_dec, 1, D_MODEL),
            ret_state_prompt[None],
            ret_state_sample[None],
            mlp_v.reshape(1, n_dec, 1, MLP_W))
```

```python
import functools

import jax
import jax.numpy as jnp
import numpy as np
from jax import lax
from jax.experimental import pallas as pl
from jax.experimental.pallas import tpu as pltpu

F32 = jnp.float32
BF16 = jnp.bfloat16

D_MODEL = 1024
DEPTH = 2
PAST_LEN = 16384
HEADS = 4
DK = D_MODEL // HEADS
DV = 2 * DK
QKW = HEADS * DK
VW = HEADS * DV
CHUNK = 128
ROPE_BASE = 10000.0
ROPE_HALF = DK // 2
MLP_W = 2 * D_MODEL
GROUPS = 8
GW = MLP_W // GROUPS
ALPHA = (2 * DEPTH) ** 0.25
LN_EPS = 1e-5
SQRT_HALF = float(np.sqrt(0.5))

TILE = 256
RET_BLOCK = TILE
VMEM_LIMIT = 60 * 1024 * 1024


def _dot(a, b):
    return jnp.dot(a, b, preferred_element_type=F32)


def _gelu(x):
    return 0.5 * x * (1.0 + lax.erf(x * SQRT_HALF))


def _silu(x):
    return x * jax.nn.sigmoid(x)


def _norm_rows(x):
    mu = jnp.mean(x, axis=-1, keepdims=True)
    d = x - mu
    var = jnp.mean(d * d, axis=-1, keepdims=True)
    return d * lax.rsqrt(var + LN_EPS)


def _rotate(h, cos, sin):
    x1 = h[:, :ROPE_HALF]
    x2 = h[:, ROPE_HALF:]
    return jnp.concatenate([x1 * cos - x2 * sin, x2 * cos + x1 * sin], axis=-1)


def _const_spec(*shape):
    return pl.BlockSpec(shape, lambda s: (0,) * len(shape), pipeline_mode=pl.Buffered(1))


def _ret_project(x_ref, cos_ref, sin_ref, win_ref, q_scr, k_scr, v_scr, sg_scr):
    xb = x_ref[...].astype(BF16)
    cos = cos_ref[...]
    sin = sin_ref[...]
    for h in range(HEADS):
        qs = slice(h * DK, (h + 1) * DK)
        q_scr[:, qs] = _rotate(_dot(xb, win_ref[:, qs]), cos, sin).astype(BF16)
        ks = slice(QKW + h * DK, QKW + (h + 1) * DK)
        k_scr[:, qs] = _rotate(_dot(xb, win_ref[:, ks]), cos, sin) * (DK ** -0.5)
    for h in range(HEADS):
        vs = slice(h * DV, (h + 1) * DV)
        v_scr[:, vs] = _dot(xb, win_ref[:, 2 * QKW + h * DV:2 * QKW + (h + 1) * DV]).astype(BF16)
        sg_scr[:, vs] = _silu(_dot(xb, win_ref[:, 2 * QKW + VW + h * DV:2 * QKW + VW + (h + 1) * DV]))


def _ret_finish(x_ref, q_scr, k_scr, v_scr, sg_scr, yin_scr, st_ref, gl_ref, decay_ref, cross_ref, kd_ref,
                gn_ref, wout_ref, lng_ref, lnb_ref, y_ref):
    for c in range(TILE // RET_BLOCK):
        rows = slice(c * RET_BLOCK, (c + 1) * RET_BLOCK)
        for h in range(HEADS):
            qs = slice(h * DK, (h + 1) * DK)
            vs = slice(h * DV, (h + 1) * DV)
            qc = q_scr[rows, qs]
            kf = k_scr[rows, qs]
            vc = v_scr[rows, vs]
            s = lax.dot_general(qc, kf.astype(BF16), (((1,), (1,)), ((), ())),
                                preferred_element_type=F32)
            s = s * decay_ref[h]
            state = st_ref[h]
            o = _dot(s.astype(BF16), vc) + _dot(qc, state.astype(BF16)) * cross_ref[h]
            kdec = (kf * kd_ref[h]).astype(BF16)
            st_ref[h] = gl_ref[h] * state + lax.dot_general(
                kdec, vc, (((0,), (0,)), ((), ())), preferred_element_type=F32)
            on = _norm_rows(o) * gn_ref[:, vs]
            yin_scr[rows, vs] = (sg_scr[rows, vs] * on).astype(BF16)

    z = ALPHA * x_ref[...] + _dot(yin_scr[...], wout_ref[...])
    y_ref[...] = _norm_rows(z) * lng_ref[...] + lnb_ref[...]


def _ret_prompt_kernel(gl_ref, x_ref, xp_ref, cos_ref, sin_ref, win_ref, decay_ref, cross_ref, kd_ref,
                       gn_ref, wout_ref, lng_ref, lnb_ref,
                       y_ref, st_ref,
                       qa, ka, va, sga, qb, kb, vb, sgb, yin_scr, *, tiles_per_seq):
    s = pl.program_id(0)
    prev = jnp.maximum(s - 1, 0)

    @pl.when(s == 0)
    def _():
        for ref in (qb, kb, vb, sgb):
            ref[...] = jnp.zeros_like(ref)

    @pl.when(prev % tiles_per_seq == 0)
    def _():
        st_ref[...] = jnp.zeros_like(st_ref)

    def step(cur, old):
        _ret_finish(xp_ref, *old, yin_scr, st_ref, gl_ref, decay_ref, cross_ref, kd_ref,
                    gn_ref, wout_ref, lng_ref, lnb_ref, y_ref)
        _ret_project(x_ref, cos_ref, sin_ref, win_ref, *cur)

    @pl.when(s % 2 == 0)
    def _():
        step((qa, ka, va, sga), (qb, kb, vb, sgb))

    @pl.when(s % 2 == 1)
    def _():
        step((qb, kb, vb, sgb), (qa, ka, va, sga))


def _ret_prompt(x, cos, sin, w_in, decay, cross, kd, gl, gn_gain, w_out, ln_g, ln_b, *, seq):
    n = x.shape[0]
    tiles = n // TILE
    tiles_per_seq = seq // TILE
    cur = lambda s: jnp.minimum(s, tiles - 1)
    prev = lambda s: jnp.maximum(s - 1, 0)
    return pl.pallas_call(
        functools.partial(_ret_prompt_kernel, tiles_per_seq=tiles_per_seq),
        grid=(tiles + 1,),
        in_specs=[
            pl.BlockSpec(memory_space=pltpu.SMEM),
            pl.BlockSpec((TILE, D_MODEL), lambda s: (cur(s), 0)),
            pl.BlockSpec((TILE, D_MODEL), lambda s: (prev(s), 0)),
            pl.BlockSpec((TILE, ROPE_HALF), lambda s: (cur(s) % tiles_per_seq, 0)),
            pl.BlockSpec((TILE, ROPE_HALF), lambda s: (cur(s) % tiles_per_seq, 0)),
            _const_spec(D_MODEL, 2 * QKW + 2 * VW),
            _const_spec(HEADS, RET_BLOCK, RET_BLOCK),
            _const_spec(HEADS, RET_BLOCK, DV),
            _const_spec(HEADS, RET_BLOCK, DK),
            _const_spec(1, VW),
            _const_spec(VW, D_MODEL),
            _const_spec(1, D_MODEL),
            _const_spec(1, D_MODEL),
        ],
        out_specs=[
            pl.BlockSpec((TILE, D_MODEL), lambda s: (prev(s), 0)),
            pl.BlockSpec((None, HEADS, DK, DV), lambda s: (prev(s) // tiles_per_seq, 0, 0, 0)),
        ],
        out_shape=[
            jax.ShapeDtypeStruct((n, D_MODEL), F32),
            jax.ShapeDtypeStruct((n // seq, HEADS, DK, DV), F32),
        ],
        scratch_shapes=2 * [
            pltpu.VMEM((TILE, QKW), BF16),
            pltpu.VMEM((TILE, QKW), F32),
            pltpu.VMEM((TILE, VW), BF16),
            pltpu.VMEM((TILE, VW), F32),
        ] + [pltpu.VMEM((TILE, VW), BF16)],
        compiler_params=pltpu.CompilerParams(
            dimension_semantics=("arbitrary",),
            vmem_limit_bytes=VMEM_LIMIT),
        name="ret_prompt",
    )(gl, x, x, cos, sin, w_in, decay, cross, kd, gn_gain, w_out, ln_g, ln_b)


def _state_update(b, valid, s0_ref, s1_ref, gam_ref, q_ref, kt_ref, v_ref, oc_ref):
    n = kt_ref.shape[-1]
    lane = lax.broadcasted_iota(jnp.int32, (DK, n), 1)
    sub = lax.broadcasted_iota(jnp.int32, (16, DK), 0)
    r16 = pl.multiple_of(lax.shift_right_logical(b, 4) * 16, 16)
    keep = jnp.logical_and(sub == b - r16, valid)
    for h in range(HEADS):
        qs = slice(h * DK, (h + 1) * DK)
        vs = slice(h * DV, (h + 1) * DV)
        s0 = s0_ref[h]
        qm = jnp.where(keep, q_ref[pl.ds(r16, 16), qs], 0.0).astype(BF16)
        oc_ref[pl.ds(r16, 16), vs] += _dot(qm, s0.astype(BF16))
        km = jnp.where(lane == b, kt_ref[h], 0.0).astype(BF16)
        s1_ref[h] = gam_ref[h] * s0 + _dot(km, v_ref[:, vs])


def _mlp_project(x_ref, win_ref, p_scr, gv_scr):
    xb = x_ref[...].astype(BF16)
    for j in range(MLP_W // DV):
        cs = slice(j * DV, (j + 1) * DV)
        u = _dot(xb, win_ref[:, j * DV:(j + 1) * DV])
        g = _dot(xb, win_ref[:, 2 * MLP_W + j * DV:2 * MLP_W + (j + 1) * DV])
        p_scr[:, cs] = _gelu(u) * _silu(g)
        gv_scr[:, cs] = _gelu(_dot(xb, win_ref[:, MLP_W + j * DV:MLP_W + (j + 1) * DV]))


def _mlp_finish(x_ref, p_scr, gv_scr, yin_scr, lngm_ref, lnbm_ref, ws_ref, bsp_ref, wout_ref,
                lng_ref, lnb_ref, y_ref):
    row = lax.broadcasted_iota(jnp.int32, (CHUNK, CHUNK), 0)
    col = lax.broadcasted_iota(jnp.int32, (CHUNK, CHUNK), 1)
    ws = [jnp.where(row >= col, ws_ref[g], 0.0).astype(BF16) for g in range(GROUPS)]
    for c in range(TILE // CHUNK):
        rows = slice(c * CHUNK, (c + 1) * CHUNK)
        vn = (_norm_rows(gv_scr[rows, :]) * lngm_ref[...] + lnbm_ref[...]).astype(BF16)
        for g in range(GROUPS):
            gs = slice(g * GW, (g + 1) * GW)
            mixed = _dot(ws[g], vn[:, gs]) + bsp_ref[:, gs]
            yin_scr[rows, gs] = (p_scr[rows, gs] * mixed).astype(BF16)

    z = ALPHA * x_ref[...] + _dot(yin_scr[...], wout_ref[...])
    y_ref[...] = _norm_rows(z) * lng_ref[...] + lnb_ref[...]


def _mlp_prompt_kernel(gam_ref, x_ref, xp_ref, win_ref, lngm_ref, lnbm_ref, ws_ref, bsp_ref, wout_ref,
                       lng_ref, lnb_ref, s0_ref, q_ref, kt_ref, v_ref,
                       y_ref, s1_ref, oc_ref,
                       pa, gva, pb, gvb, yin_scr, *, tiles, seqs):
    s = pl.program_id(0)

    @pl.when(s == 0)
    def _():
        oc_ref[...] = jnp.zeros_like(oc_ref)
        pb[...] = jnp.zeros_like(pb)
        gvb[...] = jnp.zeros_like(gvb)

    def step(cur, old):
        _mlp_finish(xp_ref, *old, yin_scr, lngm_ref, lnbm_ref, ws_ref, bsp_ref, wout_ref,
                    lng_ref, lnb_ref, y_ref)
        first = jnp.minimum(s, tiles - 1) * seqs
        for j in range(seqs):
            _state_update(first + j, s < tiles, s0_ref.at[j], s1_ref.at[j], gam_ref, q_ref, kt_ref, v_ref,
                          oc_ref)
        _mlp_project(x_ref, win_ref, *cur)

    @pl.when(s % 2 == 0)
    def _():
        step((pa, gva), (pb, gvb))

    @pl.when(s % 2 == 1)
    def _():
        step((pb, gvb), (pa, gva))


def _mlp_prompt(x, w_in, lng_m, lnb_m, w_s, bsp, w_out, ln_g, ln_b, gamma, s0, q, kt, v):
    n = x.shape[0]
    tiles = n // TILE
    n_dec = s0.shape[0]
    seqs = n_dec // tiles
    assert seqs * tiles == n_dec
    cur = lambda s: jnp.minimum(s, tiles - 1)
    prev = lambda s: jnp.maximum(s - 1, 0)
    return pl.pallas_call(
        functools.partial(_mlp_prompt_kernel, tiles=tiles, seqs=seqs),
        grid=(tiles + 1,),
        in_specs=[
            pl.BlockSpec(memory_space=pltpu.SMEM),
            pl.BlockSpec((TILE, D_MODEL), lambda s: (cur(s), 0)),
            pl.BlockSpec((TILE, D_MODEL), lambda s: (prev(s), 0)),
            _const_spec(D_MODEL, 3 * MLP_W),
            _const_spec(1, MLP_W),
            _const_spec(1, MLP_W),
            _const_spec(GROUPS, CHUNK, CHUNK),
            _const_spec(CHUNK, MLP_W),
            _const_spec(MLP_W, D_MODEL),
            _const_spec(1, D_MODEL),
            _const_spec(1, D_MODEL),
            pl.BlockSpec((seqs, HEADS, DK, DV), lambda s: (cur(s), 0, 0, 0)),
            _const_spec(n_dec, QKW),
            _const_spec(HEADS, DK, n_dec),
            _const_spec(n_dec, VW),
        ],
        out_specs=[
            pl.BlockSpec((TILE, D_MODEL), lambda s: (prev(s), 0)),
            pl.BlockSpec((seqs, HEADS, DK, DV), lambda s: (cur(s), 0, 0, 0)),
            pl.BlockSpec((n_dec, VW), lambda s: (0, 0)),
        ],
        out_shape=[
            jax.ShapeDtypeStruct((n, D_MODEL), F32),
            jax.ShapeDtypeStruct(s0.shape, F32),
            jax.ShapeDtypeStruct((n_dec, VW), F32),
        ],
        scratch_shapes=2 * [
            pltpu.VMEM((TILE, MLP_W), F32),
            pltpu.VMEM((TILE, MLP_W), F32),
        ] + [pltpu.VMEM((TILE, MLP_W), BF16)],
        compiler_params=pltpu.CompilerParams(
            dimension_semantics=("arbitrary",),
            vmem_limit_bytes=VMEM_LIMIT),
        name="mlp_prompt",
    )(gamma, x, x, w_in, lng_m, lnb_m, w_s, bsp, w_out, ln_g, ln_b, s0, q, kt, v)


def _ret_sample_in_kernel(x_ref, cos_ref, sin_ref, win_ref,
                          q_ref, kt_ref, v_ref, sg_ref, oin_ref):
    xb = x_ref[...].astype(BF16)
    cos = cos_ref[...]
    sin = sin_ref[...]
    for h in range(HEADS):
        qs = slice(h * DK, (h + 1) * DK)
        vs = slice(h * DV, (h + 1) * DV)
        q = _rotate(_dot(xb, win_ref[:, qs]), cos, sin)
        k = _rotate(_dot(xb, win_ref[:, QKW + h * DK:QKW + (h + 1) * DK]), cos, sin) * (DK ** -0.5)
        v = _dot(xb, win_ref[:, 2 * QKW + h * DV:2 * QKW + (h + 1) * DV]).astype(BF16)
        g = _dot(xb, win_ref[:, 2 * QKW + VW + h * DV:2 * QKW + VW + (h + 1) * DV])
        q_ref[:, qs] = q
        kt_ref[h] = k.T
        v_ref[:, vs] = v
        sg_ref[:, vs] = _silu(g)
        qk = jnp.sum(q.astype(BF16).astype(F32) * k.astype(BF16).astype(F32), axis=-1, keepdims=True)
        oin_ref[:, vs] = qk.astype(BF16).astype(F32) * v.astype(F32)


def _ret_sample_in(x, cos, sin, w_in):
    n = x.shape[0]
    return pl.pallas_call(
        _ret_sample_in_kernel,
        out_shape=[
            jax.ShapeDtypeStruct((n, QKW), F32),
            jax.ShapeDtypeStruct((HEADS, DK, n), F32),
            jax.ShapeDtypeStruct((n, VW), BF16),
            jax.ShapeDtypeStruct((n, VW), F32),
            jax.ShapeDtypeStruct((n, VW), F32),
        ],
        compiler_params=pltpu.CompilerParams(vmem_limit_bytes=VMEM_LIMIT),
        name="ret_sample_in",
    )(x, cos, sin, w_in)


def _sample_finish_kernel(gam_ref, x_ref, oin_ref, oc_ref, sg_ref, gn_ref, wout0_ref, lng0_ref, lnb0_ref,
                          win_ref, lngm_ref, lnbm_ref, wsd_ref, bsd_ref, wout1_ref, lng1_ref, lnb1_ref,
                          y_ref, vout_ref, yin_scr):
    x = x_ref[...]
    for h in range(HEADS):
        vs = slice(h * DV, (h + 1) * DV)
        o = oin_ref[:, vs] + oc_ref[:, vs] * gam_ref[h]
        yin_scr[:, vs] = (sg_ref[:, vs] * (_norm_rows(o) * gn_ref[:, vs])).astype(BF16)
    z = ALPHA * x + _dot(yin_scr[...], wout0_ref[...])
    x1 = _norm_rows(z) * lng0_ref[...] + lnb0_ref[...]

    xb = x1.astype(BF16)
    u = _gelu(_dot(xb, win_ref[:, :MLP_W]))
    v = _gelu(_dot(xb, win_ref[:, MLP_W:2 * MLP_W]))
    g = _dot(xb, win_ref[:, 2 * MLP_W:])
    vn = _norm_rows(v) * lngm_ref[...] + lnbm_ref[...]
    vout_ref[...] = vn
    mixed = wsd_ref[...].astype(BF16).astype(F32) * vn.astype(BF16).astype(F32) + bsd_ref[...]
    yin = (u * mixed * _silu(g)).astype(BF16)
    z1 = ALPHA * x1 + _dot(yin, wout1_ref[...])
    y_ref[...] = _norm_rows(z1) * lng1_ref[...] + lnb1_ref[...]


def _sample_finish(gamma, x, oin, oc, sg, gn_gain, w_out0, lng0, lnb0,
                   w_in1, lng_m, lnb_m, wsd, bsd, w_out1, lng1, lnb1):
    n = x.shape[0]
    vmem = pl.BlockSpec(memory_space=pltpu.VMEM)
    return pl.pallas_call(
        _sample_finish_kernel,
        in_specs=[pl.BlockSpec(memory_space=pltpu.SMEM)] + [vmem] * 16,
        out_shape=[
            jax.ShapeDtypeStruct((n, D_MODEL), F32),
            jax.ShapeDtypeStruct((n, MLP_W), F32),
        ],
        scratch_shapes=[pltpu.VMEM((n, VW), BF16)],
        compiler_params=pltpu.CompilerParams(vmem_limit_bytes=VMEM_LIMIT),
        name="sample_finish",
    )(gamma, x, oin, oc, sg, gn_gain, w_out0, lng0, lnb0,
      w_in1, lng_m, lnb_m, wsd, bsd, w_out1, lng1, lnb1)


def _rope_tables(pos):
    inv = ROPE_BASE ** (-jnp.arange(ROPE_HALF, dtype=F32) / ROPE_HALF)
    ang = pos.astype(F32)[:, None] * inv[None, :]
    return jnp.cos(ang), jnp.sin(ang)


def _decay_tables():
    lg = jnp.log1p(-jnp.exp2(-5.0 - jnp.arange(HEADS, dtype=F32)))
    idx = jnp.arange(RET_BLOCK, dtype=F32)
    diff = idx[:, None] - idx[None, :]
    decay = jnp.where(diff >= 0, jnp.exp(jnp.maximum(diff, 0.0)[None] * lg[:, None, None]), 0.0)
    cross = jnp.exp((idx[:, None] + 1.0) * lg[None, :])
    kd = jnp.exp((RET_BLOCK - 1.0 - idx)[:, None] * lg[None, :])
    cross = jnp.broadcast_to(cross.T[:, :, None], (HEADS, RET_BLOCK, DV))
    kd = jnp.broadcast_to(kd.T[:, :, None], (HEADS, RET_BLOCK, DK))
    return decay, cross, kd, jnp.exp(RET_BLOCK * lg), jnp.exp(lg)


def kernel(x_prompt, x_sample, state_ret, ln_gain, ln_bias, w_in_ret, gn_gain_ret, w_out_ret,
           w_in_mlp, ln_gain_mlp, ln_bias_mlp, w_spatial, b_spatial, w_out_mlp):
    batch, seq, _ = x_prompt.shape
    n_dec = x_sample.shape[0]
    assert x_sample.shape[1] == 1 and seq % TILE == 0

    w_in0 = w_in_ret[0].astype(BF16)
    w_out0 = w_out_ret[0].astype(BF16)
    w_in1 = w_in_mlp[0].astype(BF16)
    w_out1 = w_out_mlp[0].astype(BF16)
    lng0, lnb0 = ln_gain[0][None], ln_bias[0][None]
    lng1, lnb1 = ln_gain[1][None], ln_bias[1][None]
    gn_gain = gn_gain_ret[0][None]
    lng_m, lnb_m = ln_gain_mlp[0][None], ln_bias_mlp[0][None]

    cos_p, sin_p = _rope_tables(jnp.arange(seq, dtype=jnp.int32))
    cos_s, sin_s = _rope_tables(PAST_LEN + jnp.arange(1, dtype=jnp.int32))
    decay, cross, kd, gamma_chunk, gamma = _decay_tables()

    xs = x_sample.reshape(n_dec, D_MODEL)
    q, kt, v, sg, oin = _ret_sample_in(xs, cos_s, sin_s, w_in0)

    x1, ret_state_prompt = _ret_prompt(x_prompt.reshape(batch * seq, D_MODEL), cos_p, sin_p, w_in0,
                                       decay, cross, kd, gamma_chunk, gn_gain, w_out0, lng0, lnb0, seq=seq)
    bsp = jnp.repeat(jnp.transpose(b_spatial[0]), GW, axis=1)
    y_prompt, ret_state_sample, oc = _mlp_prompt(
        x1, w_in1, lng_m, lnb_m, w_spatial[0], bsp, w_out1, lng1, lnb1, gamma, state_ret[0], q, kt, v)

    wsd = jnp.repeat(w_spatial[0, :, 0, 0], GW)[None]
    bsd = jnp.repeat(b_spatial[0, :, 0], GW)[None]
    y_sample, mlp_v = _sample_finish(gamma, xs, oin, oc, sg, gn_gain, w_out0, lng0, lnb0,
                                     w_in1, lng_m, lnb_m, wsd, bsd, w_out1, lng1, lnb1)

    return (y_prompt.reshape(batch, seq, D_MODEL),
            y_sample.reshape(n_dec, 1, D_MODEL),
            ret_state_prompt[None],
            ret_state_sample[None],
            mlp_v.reshape(1, n_dec, 1, MLP_W))
```

```python
import functools

import jax
import jax.numpy as jnp
import numpy as np
from jax import lax
from jax.experimental import pallas as pl
from jax.experimental.pallas import tpu as pltpu
from jax.experimental.pallas import tpu_sc as plsc

F32 = jnp.float32
BF16 = jnp.bfloat16

D_MODEL = 1024
DEPTH = 2
PAST_LEN = 16384
HEADS = 4
DK = D_MODEL // HEADS
DV = 2 * DK
QKW = HEADS * DK
VW = HEADS * DV
CHUNK = 128
ROPE_BASE = 10000.0
ROPE_HALF = DK // 2
MLP_W = 2 * D_MODEL
GROUPS = 8
GW = MLP_W // GROUPS
ALPHA = (2 * DEPTH) ** 0.25
LN_EPS = 1e-5
SQRT_HALF = float(np.sqrt(0.5))

TILE = 256
RET_BLOCK = TILE
MLP_TILE = 256
RET_STAGE_ORDER = ("qk0", "qk1", "qk2", "qk3", "ret0", "ret1", "ret2", "ret3", "out",
                   "gate0", "gate1", "gate2", "gate3", "v0", "v1", "v2", "v3")
MLP_STAGE_ORDER = ("gate0", "v0", "mix0", "gate1", "v1", "mix1", "gate2", "v2", "out", "gate3", "v3")
VMEM_LIMIT = 60 * 1024 * 1024


def _dot(a, b):
    return jnp.dot(a, b, preferred_element_type=F32)


def _gelu(x):
    return 0.5 * x * (1.0 + lax.erf(x * SQRT_HALF))


def _silu(x):
    return x * jax.nn.sigmoid(x)


def _norm_rows(x):
    mu = jnp.mean(x, axis=-1, keepdims=True)
    d = x - mu
    var = jnp.mean(d * d, axis=-1, keepdims=True)
    return d * lax.rsqrt(var + LN_EPS)


def _rotate(h, cos, sin):
    x1 = h[:, :ROPE_HALF]
    x2 = h[:, ROPE_HALF:]
    return jnp.concatenate([x1 * cos - x2 * sin, x2 * cos + x1 * sin], axis=-1)


def _const_spec(*shape):
    return pl.BlockSpec(shape, lambda s: (0,) * len(shape), pipeline_mode=pl.Buffered(1))


def _ret_project_qk(h, xb, cos, sin, win_ref, q_scr, k_scr):
    qs = slice(h * DK, (h + 1) * DK)
    q_scr[:, qs] = _rotate(_dot(xb, win_ref[:, qs]), cos, sin).astype(BF16)
    ks = slice(QKW + h * DK, QKW + (h + 1) * DK)
    k_scr[:, qs] = _rotate(_dot(xb, win_ref[:, ks]), cos, sin) * (DK ** -0.5)


def _ret_project_gate(h, xb, win_ref, sg_scr):
    vs = slice(h * DV, (h + 1) * DV)
    sg_scr[:, vs] = _silu(_dot(xb, win_ref[:, 2 * QKW + VW + h * DV:2 * QKW + VW + (h + 1) * DV]))


def _ret_project_v(h, xb, win_ref, v_scr):
    vs = slice(h * DV, (h + 1) * DV)
    v_scr[:, vs] = _dot(xb, win_ref[:, 2 * QKW + h * DV:2 * QKW + (h + 1) * DV]).astype(BF16)


def _ret_head(h, q_scr, k_scr, v_scr, sg_scr, yin_scr, st_ref, gl_ref, decay_ref, cross_ref, kd_ref, gn_ref):
    qs = slice(h * DK, (h + 1) * DK)
    vs = slice(h * DV, (h + 1) * DV)
    for c in range(TILE // RET_BLOCK):
        rows = slice(c * RET_BLOCK, (c + 1) * RET_BLOCK)
        qc = q_scr[rows, qs]
        kf = k_scr[rows, qs]
        vc = v_scr[rows, vs]
        s = lax.dot_general(qc, kf.astype(BF16), (((1,), (1,)), ((), ())),
                            preferred_element_type=F32)
        s = s * decay_ref[h]
        state = st_ref[h]
        o = _dot(s.astype(BF16), vc) + _dot(qc, state.astype(BF16)) * jnp.tile(cross_ref[h], (1, DV // 128))
        kdec = (kf * jnp.tile(kd_ref[h], (1, DK // 128))).astype(BF16)
        st_ref[h] = gl_ref[h] * state + lax.dot_general(
            kdec, vc, (((0,), (0,)), ((), ())), preferred_element_type=F32)
        on = _norm_rows(o) * gn_ref[:, vs]
        yin_scr[rows, vs] = (sg_scr[rows, vs] * on).astype(BF16)


def _ret_out(x_ref, yin_scr, wout_ref, lng_ref, lnb_ref, y_ref):
    z = ALPHA * x_ref[...] + _dot(yin_scr[...], wout_ref[...])
    y_ref[...] = _norm_rows(z) * lng_ref[...] + lnb_ref[...]


def _ret_prompt_kernel(gl_ref, x_ref, xp_ref, cosr_ref, sinr_ref, cost_ref, sint_ref, win_ref,
                       decay_ref, cross_ref, kd_ref, gn_ref, wout_ref, lng_ref, lnb_ref, wa_ref, wb_ref,
                       y_ref, st_ref, wa16_ref, wb16_ref,
                       qa, ka, va, sga, qb, kb, vb, sgb, yin_scr, *, tiles, tiles_per_seq):
    s = pl.program_id(0)
    prev = jnp.maximum(s - 1, 0)
    tile_in_seq = jnp.minimum(s, tiles - 1) % tiles_per_seq

    wa16_ref[...] = wa_ref[...].astype(BF16)
    wb16_ref[...] = wb_ref[...].astype(BF16)

    @pl.when(prev % tiles_per_seq == 0)
    def _():
        st_ref[...] = jnp.zeros_like(st_ref)

    def step(cur, old):
        stages = {}
        if cur is not None:
            ct = cost_ref[pl.ds(tile_in_seq, 1), :]
            st = sint_ref[pl.ds(tile_in_seq, 1), :]
            cos = ct * cosr_ref[...] - st * sinr_ref[...]
            sin = st * cosr_ref[...] + ct * sinr_ref[...]
            xb = x_ref[...].astype(BF16)
            for h in range(HEADS):
                stages[f"qk{h}"] = functools.partial(_ret_project_qk, h, xb, cos, sin, win_ref, *cur[:2])
                stages[f"gate{h}"] = functools.partial(_ret_project_gate, h, xb, win_ref, cur[3])
                stages[f"v{h}"] = functools.partial(_ret_project_v, h, xb, win_ref, cur[2])
        if old is not None:
            for h in range(HEADS):
                stages[f"ret{h}"] = functools.partial(_ret_head, h, *old, yin_scr, st_ref, gl_ref, decay_ref,
                                                      cross_ref, kd_ref, gn_ref)
            stages["out"] = functools.partial(_ret_out, xp_ref, yin_scr, wout_ref, lng_ref, lnb_ref, y_ref)
        assert set(stages) <= set(RET_STAGE_ORDER)
        for name in RET_STAGE_ORDER:
            if name in stages:
                stages[name]()

    sets = ((qa, ka, va, sga), (qb, kb, vb, sgb))

    @pl.when(s == 0)
    def _():
        step(sets[0], None)

    for parity in range(2):
        @pl.when(jnp.logical_and(jnp.logical_and(s > 0, s < tiles), s % 2 == parity))
        def _():
            step(sets[parity], sets[1 - parity])

    @pl.when(s == tiles)
    def _():
        step(None, sets[(tiles - 1) % 2])


def _ret_prompt(x, rope, w_in, decay, cross, kd, gl, gn_gain, w_out, ln_g, ln_b, w_next_a, w_next_b, *, seq):
    cos_r, sin_r, cos_t, sin_t = rope
    n = x.shape[0]
    tiles = n // TILE
    tiles_per_seq = seq // TILE
    cur = lambda s: jnp.minimum(s, tiles - 1)
    prev = lambda s: jnp.maximum(s - 1, 0)
    slab_a = w_next_a.shape[0] // tiles
    slab_b = w_next_b.shape[0] // tiles
    assert slab_a * tiles == w_next_a.shape[0] and slab_b * tiles == w_next_b.shape[0]
    return pl.pallas_call(
        functools.partial(_ret_prompt_kernel, tiles=tiles, tiles_per_seq=tiles_per_seq),
        grid=(tiles + 1,),
        in_specs=[
            pl.BlockSpec(memory_space=pltpu.SMEM),
            pl.BlockSpec((TILE, D_MODEL), lambda s: (cur(s), 0)),
            pl.BlockSpec((TILE, D_MODEL), lambda s: (prev(s), 0)),
            _const_spec(TILE, ROPE_HALF),
            _const_spec(TILE, ROPE_HALF),
            _const_spec(tiles_per_seq, ROPE_HALF),
            _const_spec(tiles_per_seq, ROPE_HALF),
            _const_spec(D_MODEL, 2 * QKW + 2 * VW),
            _const_spec(HEADS, RET_BLOCK, RET_BLOCK),
            _const_spec(HEADS, RET_BLOCK, 128),
            _const_spec(HEADS, RET_BLOCK, 128),
            _const_spec(1, VW),
            _const_spec(VW, D_MODEL),
            _const_spec(1, D_MODEL),
            _const_spec(1, D_MODEL),
            pl.BlockSpec((slab_a, w_next_a.shape[1]), lambda s: (cur(s), 0)),
            pl.BlockSpec((slab_b, w_next_b.shape[1]), lambda s: (cur(s), 0)),
        ],
        out_specs=[
            pl.BlockSpec((TILE, D_MODEL), lambda s: (prev(s), 0)),
            pl.BlockSpec((None, HEADS, DK, DV), lambda s: (prev(s) // tiles_per_seq, 0, 0, 0)),
            pl.BlockSpec((slab_a, w_next_a.shape[1]), lambda s: (cur(s), 0)),
            pl.BlockSpec((slab_b, w_next_b.shape[1]), lambda s: (cur(s), 0)),
        ],
        out_shape=[
            jax.ShapeDtypeStruct((n, D_MODEL), F32),
            jax.ShapeDtypeStruct((n // seq, HEADS, DK, DV), F32),
            jax.ShapeDtypeStruct(w_next_a.shape, BF16),
            jax.ShapeDtypeStruct(w_next_b.shape, BF16),
        ],
        scratch_shapes=2 * [
            pltpu.VMEM((TILE, QKW), BF16),
            pltpu.VMEM((TILE, QKW), F32),
            pltpu.VMEM((TILE, VW), BF16),
            pltpu.VMEM((TILE, VW), F32),
        ] + [pltpu.VMEM((TILE, VW), BF16)],
        compiler_params=pltpu.CompilerParams(
            dimension_semantics=("arbitrary",),
            vmem_limit_bytes=VMEM_LIMIT),
        name="ret_prompt",
    )(gl, x, x, cos_r, sin_r, cos_t, sin_t, w_in, decay, cross, kd, gn_gain, w_out, ln_g, ln_b,
      w_next_a, w_next_b)


def _mlp_project_gate(j, xb, win_ref, p_scr):
    u = _dot(xb, win_ref[:, j * DV:(j + 1) * DV])
    g = _dot(xb, win_ref[:, 2 * MLP_W + j * DV:2 * MLP_W + (j + 1) * DV])
    p_scr[:, j * DV:(j + 1) * DV] = _gelu(u) * _silu(g)


def _mlp_project_v(j, xb, win_ref, gv_scr):
    gv_scr[:, j * DV:(j + 1) * DV] = _gelu(_dot(xb, win_ref[:, MLP_W + j * DV:MLP_W + (j + 1) * DV]))


def _causal_mix_weights(ws_ref):
    row = lax.broadcasted_iota(jnp.int32, (CHUNK, CHUNK), 0)
    col = lax.broadcasted_iota(jnp.int32, (CHUNK, CHUNK), 1)
    return [jnp.where(row >= col, ws_ref[g], 0.0).astype(BF16) for g in range(GROUPS)]


def _mlp_mix(c, ws, p_scr, gv_scr, yin_scr, lngm_ref, lnbm_ref, bsp_ref):
    rows = slice(c * CHUNK, (c + 1) * CHUNK)
    vn = (_norm_rows(gv_scr[rows, :]) * lngm_ref[...] + lnbm_ref[...]).astype(BF16)
    for g in range(GROUPS):
        gs = slice(g * GW, (g + 1) * GW)
        bias = jnp.tile(bsp_ref[:, g * 128:(g + 1) * 128], (1, GW // 128))
        mixed = _dot(ws[g], vn[:, gs]) + bias
        yin_scr[rows, gs] = (p_scr[rows, gs] * mixed).astype(BF16)


def _mlp_out(x_ref, yin_scr, wout_ref, lng_ref, lnb_ref, y_ref):
    z = ALPHA * x_ref[...] + _dot(yin_scr[...], wout_ref[...])
    y_ref[...] = _norm_rows(z) * lng_ref[...] + lnb_ref[...]


def _mlp_prompt_kernel(x_ref, xp_ref, win_ref, lngm_ref, lnbm_ref, ws_ref, bsp_ref, wout_ref,
                       lng_ref, lnb_ref,
                       y_ref,
                       pa, gva, pb, gvb, yin_scr, *, tiles):
    s = pl.program_id(0)

    def step(cur, old):
        stages = {}
        if cur is not None:
            xb = x_ref[...].astype(BF16)
            for j in range(MLP_W // DV):
                stages[f"gate{j}"] = functools.partial(_mlp_project_gate, j, xb, win_ref, cur[0])
                stages[f"v{j}"] = functools.partial(_mlp_project_v, j, xb, win_ref, cur[1])
        if old is not None:
            ws = _causal_mix_weights(ws_ref)
            for c in range(MLP_TILE // CHUNK):
                stages[f"mix{c}"] = functools.partial(_mlp_mix, c, ws, *old, yin_scr, lngm_ref, lnbm_ref,
                                                      bsp_ref)
            stages["out"] = functools.partial(_mlp_out, xp_ref, yin_scr, wout_ref, lng_ref, lnb_ref, y_ref)
        assert set(stages) <= set(MLP_STAGE_ORDER)
        for name in MLP_STAGE_ORDER:
            if name in stages:
                stages[name]()

    sets = ((pa, gva), (pb, gvb))

    @pl.when(s == 0)
    def _():
        step(sets[0], None)

    for parity in range(2):
        @pl.when(jnp.logical_and(jnp.logical_and(s > 0, s < tiles), s % 2 == parity))
        def _():
            step(sets[parity], sets[1 - parity])

    @pl.when(s == tiles)
    def _():
        step(None, sets[(tiles - 1) % 2])


def _mlp_prompt(x, w_in, lng_m, lnb_m, w_s, bsp, w_out, ln_g, ln_b):
    n = x.shape[0]
    tiles = n // MLP_TILE
    cur = lambda s: jnp.minimum(s, tiles - 1)
    prev = lambda s: jnp.maximum(s - 1, 0)
    return pl.pallas_call(
        functools.partial(_mlp_prompt_kernel, tiles=tiles),
        grid=(tiles + 1,),
        in_specs=[
            pl.BlockSpec((MLP_TILE, D_MODEL), lambda s: (cur(s), 0)),
            pl.BlockSpec((MLP_TILE, D_MODEL), lambda s: (prev(s), 0)),
            _const_spec(D_MODEL, 3 * MLP_W),
            _const_spec(1, MLP_W),
            _const_spec(1, MLP_W),
            _const_spec(GROUPS, CHUNK, CHUNK),
            _const_spec(CHUNK, GROUPS * 128),
            _const_spec(MLP_W, D_MODEL),
            _const_spec(1, D_MODEL),
            _const_spec(1, D_MODEL),
        ],
        out_specs=pl.BlockSpec((MLP_TILE, D_MODEL), lambda s: (prev(s), 0)),
        out_shape=jax.ShapeDtypeStruct((n, D_MODEL), F32),
        scratch_shapes=2 * [
            pltpu.VMEM((MLP_TILE, MLP_W), F32),
            pltpu.VMEM((MLP_TILE, MLP_W), F32),
        ] + [pltpu.VMEM((MLP_TILE, MLP_W), BF16)],
        compiler_params=pltpu.CompilerParams(
            dimension_semantics=("arbitrary",),
            vmem_limit_bytes=VMEM_LIMIT),
        name="mlp_prompt",
    )(x, x, w_in, lng_m, lnb_m, w_s, bsp, w_out, ln_g, ln_b)


SAMPLE_COLS = 1024


def _ret_sample_in_kernel(x_ref, cos_ref, sin_ref, win_ref, wo_ref,
                          q_ref, k_ref, v_ref, sg_ref, oin_ref, win16_ref, wo16_ref, xb_scr):
    i = pl.program_id(0)

    @pl.when(i == 0)
    def _():
        xb_scr[...] = x_ref[:, 0, :].astype(BF16)

    w16 = win_ref[...].astype(BF16)
    win16_ref[...] = w16
    wo16_ref[...] = wo_ref[...].astype(BF16)
    h = _dot(xb_scr[...], w16)
    cos = cos_ref[...]
    sin = sin_ref[...]
    q_steps = QKW // SAMPLE_COLS
    v_steps = VW // SAMPLE_COLS
    qk_heads = SAMPLE_COLS // DK
    v_heads = SAMPLE_COLS // DV

    @pl.when(i < q_steps)
    def _():
        for j in range(qk_heads):
            q_ref[i * qk_heads + j] = _rotate(h[:, j * DK:(j + 1) * DK], cos, sin)

    @pl.when(jnp.logical_and(i >= q_steps, i < 2 * q_steps))
    def _():
        for j in range(qk_heads):
            k_ref[(i - q_steps) * qk_heads + j] = _rotate(h[:, j * DK:(j + 1) * DK], cos, sin) * (DK ** -0.5)

    @pl.when(jnp.logical_and(i >= 2 * q_steps, i < 2 * q_steps + v_steps))
    def _():
        for j in range(v_heads):
            v_ref[(i - 2 * q_steps) * v_heads + j] = h[:, j * DV:(j + 1) * DV]

    @pl.when(i >= 2 * q_steps + v_steps)
    def _():
        for j in range(v_heads):
            sg_ref[(i - 2 * q_steps - v_steps) * v_heads + j] = _silu(h[:, j * DV:(j + 1) * DV])

    @pl.when(i == pl.num_programs(0) - 1)
    def _():
        for hd in range(HEADS):
            qk = jnp.sum(q_ref[hd].astype(BF16).astype(F32) * k_ref[hd].astype(BF16).astype(F32),
                         axis=-1, keepdims=True)
            oin_ref[hd] = qk.astype(BF16).astype(F32) * v_ref[hd].astype(BF16).astype(F32)


def _ret_sample_in(x, cos, sin, w_in, w_out):
    n = x.shape[0]
    steps = w_in.shape[1] // SAMPLE_COLS
    wo_rows = 512
    wo_steps = w_out.shape[0] // wo_rows
    assert QKW % SAMPLE_COLS == 0 and SAMPLE_COLS % DV == 0 and wo_steps <= steps
    whole = lambda *shape: pl.BlockSpec(shape, lambda i: (0,) * len(shape))
    wo_spec = pl.BlockSpec((wo_rows, w_out.shape[1]), lambda i: (jnp.minimum(i, wo_steps - 1), 0))
    return pl.pallas_call(
        _ret_sample_in_kernel,
        grid=(steps,),
        in_specs=[
            whole(n, 1, D_MODEL),
            whole(1, ROPE_HALF),
            whole(1, ROPE_HALF),
            pl.BlockSpec((D_MODEL, SAMPLE_COLS), lambda i: (0, i)),
            wo_spec,
        ],
        out_specs=[
            whole(HEADS, n, DK),
            whole(HEADS, n, DK),
            whole(HEADS, n, DV),
            whole(HEADS, n, DV),
            whole(HEADS, n, DV),
            pl.BlockSpec((D_MODEL, SAMPLE_COLS), lambda i: (0, i)),
            wo_spec,
        ],
        out_shape=[
            jax.ShapeDtypeStruct((HEADS, n, DK), F32),
            jax.ShapeDtypeStruct((HEADS, n, DK), F32),
            jax.ShapeDtypeStruct((HEADS, n, DV), F32),
            jax.ShapeDtypeStruct((HEADS, n, DV), F32),
            jax.ShapeDtypeStruct((HEADS, n, DV), F32),
            jax.ShapeDtypeStruct(w_in.shape, BF16),
            jax.ShapeDtypeStruct(w_out.shape, BF16),
        ],
        scratch_shapes=[pltpu.VMEM((n, D_MODEL), BF16)],
        compiler_params=pltpu.CompilerParams(
            dimension_semantics=("arbitrary",),
            vmem_limit_bytes=VMEM_LIMIT),
        name="ret_sample_in",
    )(x, cos, sin, w_in, w_out)


SC_LANES = 16
SC_ROWS = 32
SC_HALF = 256


def _state_sc_kernel(gam_hbm, s0_hbm, k_hbm, q_hbm, v_hbm, s1_hbm, o_hbm,
                     gam_v, v_v, o_v, k_v, q_v, s0_v, s1_v, *, pairs_per_worker, num_cores, n):
    wid = lax.axis_index("s") * num_cores + lax.axis_index("c")
    first = wid * pairs_per_worker
    head = first // n
    pltpu.sync_copy(gam_hbm, gam_v)
    pltpu.sync_copy(v_hbm.at[pl.ds(first, pairs_per_worker)], v_v)
    pltpu.sync_copy(k_hbm.at[pl.ds(first, pairs_per_worker)], k_v)
    pltpu.sync_copy(q_hbm.at[pl.ds(first, pairs_per_worker)], q_v)
    gam = gam_v[head, pl.ds(0, SC_LANES)]

    @pl.loop(0, pairs_per_worker)
    def _(pi):
        seq = first + pi - head * n
        pi_vec = jnp.full((SC_LANES,), pi, jnp.int32)
        for c in range(DV // SC_LANES):
            o_v[pi, pl.ds(c * SC_LANES, SC_LANES)] = jnp.zeros((SC_LANES,), F32)

        @pl.loop(0, DK // SC_ROWS)
        def _(blk):
            row0 = (seq * HEADS + head) * DK + blk * SC_ROWS
            pltpu.sync_copy(s0_hbm.at[pl.ds(row0, SC_ROWS)], s0_v)
            for half in range(DV // SC_HALF):
                cols = [half * SC_HALF + c * SC_LANES for c in range(SC_HALF // SC_LANES)]
                vregs = [v_v[pi, pl.ds(c, SC_LANES)] for c in cols]

                @pl.loop(0, SC_ROWS // 8)
                def _(grp):
                    acc = [None] * len(cols)
                    for j in range(8):
                        d = grp * 8 + j
                        d_vec = jnp.full((SC_LANES,), blk * SC_ROWS + d, jnp.int32)
                        kv = plsc.load_gather(k_v, [pi_vec, d_vec])
                        qv = plsc.load_gather(q_v, [pi_vec, d_vec])
                        for i, (c, vreg) in enumerate(zip(cols, vregs)):
                            s0 = s0_v[d, pl.ds(c, SC_LANES)]
                            s1_v[d, pl.ds(c, SC_LANES)] = gam * s0 + kv * vreg
                            acc[i] = qv * s0 if j == 0 else acc[i] + qv * s0
                    for c, a in zip(cols, acc):
                        plsc.addupdate(o_v.at[pi, pl.ds(c, SC_LANES)], a)
            pltpu.sync_copy(s1_v, s1_hbm.at[pl.ds(row0, SC_ROWS)])

    pltpu.sync_copy(o_v, o_hbm.at[pl.ds(first, pairs_per_worker)])


def _state_sc(gamma, s0, q, k, v):
    n = s0.shape[0]
    info = plsc.get_sparse_core_info()
    workers = info.num_cores * info.num_subcores
    pairs = n * HEADS
    pairs_per_worker = pairs // workers
    assert info.num_lanes == SC_LANES and pairs % workers == 0 and n % pairs_per_worker == 0
    gam_rows = jnp.broadcast_to(jnp.pad(gamma, (0, 8 - HEADS))[:, None], (8, 128))
    mesh = plsc.VectorSubcoreMesh(core_axis_name="c", subcore_axis_name="s")
    s1, o = pl.kernel(
        functools.partial(_state_sc_kernel, pairs_per_worker=pairs_per_worker, num_cores=info.num_cores, n=n),
        out_type=[
            jax.ShapeDtypeStruct((pairs * DK, DV), F32),
            jax.ShapeDtypeStruct((pairs, DV), F32),
        ],
        mesh=mesh,
        scratch_types=[
            pltpu.VMEM((8, 128), F32),
            pltpu.VMEM((pairs_per_worker, DV), F32),
            pltpu.VMEM((pairs_per_worker, DV), F32),
            pltpu.VMEM((pairs_per_worker, DK), F32),
            pltpu.VMEM((pairs_per_worker, DK), F32),
            pltpu.VMEM((SC_ROWS, DV), F32),
            pltpu.VMEM((SC_ROWS, DV), F32),
        ],
        compiler_params=pltpu.CompilerParams(use_tc_tiling_on_sc=True, needs_layout_passes=False),
        name="state_sc",
    )(gam_rows, s0.reshape(pairs * DK, DV), k.reshape(pairs, DK), q.reshape(pairs, DK), v.reshape(pairs, DV))
    return s1.reshape(s0.shape), o.reshape(HEADS, n, DV)


def _sample_finish_kernel(gam_ref, x_ref, oin_ref, oc_ref, sg_ref, gn_ref, wout0_ref, lng0_ref, lnb0_ref,
                          win_ref, lngm_ref, lnbm_ref, wsd_ref, bsd_ref, wout1_ref, lng1_ref, lnb1_ref,
                          y_ref, vout_ref, yin_scr):
    x = x_ref[:, 0, :]
    for h in range(HEADS):
        vs = slice(h * DV, (h + 1) * DV)
        o = oin_ref[h] + oc_ref[h] * gam_ref[h]
        yin_scr[:, vs] = (sg_ref[h] * (_norm_rows(o) * gn_ref[:, vs])).astype(BF16)
    z = ALPHA * x + _dot(yin_scr[...], wout0_ref[...])
    x1 = _norm_rows(z) * lng0_ref[...] + lnb0_ref[...]

    xb = x1.astype(BF16)
    u = _gelu(_dot(xb, win_ref[:, :MLP_W]))
    v = _gelu(_dot(xb, win_ref[:, MLP_W:2 * MLP_W]))
    g = _dot(xb, win_ref[:, 2 * MLP_W:])
    vn = _norm_rows(v) * lngm_ref[...] + lnbm_ref[...]
    vout_ref[0, :, 0, :] = vn
    mixed = wsd_ref[...].astype(BF16).astype(F32) * vn.astype(BF16).astype(F32) + bsd_ref[...]
    yin = (u * mixed * _silu(g)).astype(BF16)
    z1 = ALPHA * x1 + _dot(yin, wout1_ref[...])
    y_ref[:, 0, :] = _norm_rows(z1) * lng1_ref[...] + lnb1_ref[...]


def _sample_finish(gamma, x, oin, oc, sg, gn_gain, w_out0, lng0, lnb0,
                   w_in1, lng_m, lnb_m, wsd, bsd, w_out1, lng1, lnb1):
    n = x.shape[0]
    vmem = pl.BlockSpec(memory_space=pltpu.VMEM)
    return pl.pallas_call(
        _sample_finish_kernel,
        in_specs=[pl.BlockSpec(memory_space=pltpu.SMEM)] + [vmem] * 16,
        out_shape=[
            jax.ShapeDtypeStruct((n, 1, D_MODEL), F32),
            jax.ShapeDtypeStruct((1, n, 1, MLP_W), F32),
        ],
        scratch_shapes=[pltpu.VMEM((n, VW), BF16)],
        compiler_params=pltpu.CompilerParams(vmem_limit_bytes=VMEM_LIMIT),
        name="sample_finish",
    )(gamma, x, oin, oc, sg, gn_gain, w_out0, lng0, lnb0,
      w_in1, lng_m, lnb_m, wsd, bsd, w_out1, lng1, lnb1)


def _rope_tables(pos):
    inv = ROPE_BASE ** (-jnp.arange(ROPE_HALF, dtype=F32) / ROPE_HALF)
    ang = pos.astype(F32)[:, None] * inv[None, :]
    return jnp.cos(ang), jnp.sin(ang)


def _decay_tables():
    lg = jnp.log1p(-jnp.exp2(-5.0 - jnp.arange(HEADS, dtype=F32)))
    idx = jnp.arange(RET_BLOCK, dtype=F32)
    diff = idx[:, None] - idx[None, :]
    decay = jnp.where(diff >= 0, jnp.exp(jnp.maximum(diff, 0.0)[None] * lg[:, None, None]), 0.0)
    cross = jnp.exp((idx[:, None] + 1.0) * lg[None, :])
    kd = jnp.exp((RET_BLOCK - 1.0 - idx)[:, None] * lg[None, :])
    cross = jnp.broadcast_to(cross.T[:, :, None], (HEADS, RET_BLOCK, 128))
    kd = jnp.broadcast_to(kd.T[:, :, None], (HEADS, RET_BLOCK, 128))
    return decay, cross, kd, jnp.exp(RET_BLOCK * lg), jnp.exp(lg)


def kernel(x_prompt, x_sample, state_ret, ln_gain, ln_bias, w_in_ret, gn_gain_ret, w_out_ret,
           w_in_mlp, ln_gain_mlp, ln_bias_mlp, w_spatial, b_spatial, w_out_mlp):
    batch, seq, _ = x_prompt.shape
    n_dec = x_sample.shape[0]
    assert x_sample.shape[1] == 1 and seq % TILE == 0 and seq % MLP_TILE == 0

    lng0, lnb0 = ln_gain[0][None], ln_bias[0][None]
    lng1, lnb1 = ln_gain[1][None], ln_bias[1][None]
    gn_gain = gn_gain_ret[0][None]
    lng_m, lnb_m = ln_gain_mlp[0][None], ln_bias_mlp[0][None]

    rope = (*_rope_tables(jnp.arange(TILE, dtype=jnp.int32)),
            *_rope_tables(TILE * jnp.arange(seq // TILE, dtype=jnp.int32)))
    cos_s, sin_s = _rope_tables(PAST_LEN + jnp.arange(1, dtype=jnp.int32))
    decay, cross, kd, gamma_chunk, gamma = _decay_tables()

    q, k, v, sg, oin, w_in0, w_out0 = _ret_sample_in(x_sample, cos_s, sin_s, w_in_ret[0], w_out_ret[0])
    ret_state_sample, oc = _state_sc(gamma, state_ret[0], q, k, v)

    x1, ret_state_prompt, w_in1, w_out1 = _ret_prompt(
        x_prompt.reshape(batch * seq, D_MODEL), rope, w_in0, decay, cross, kd, gamma_chunk,
        gn_gain, w_out0, lng0, lnb0, w_in_mlp[0], w_out_mlp[0], seq=seq)
    bsp = jnp.repeat(jnp.transpose(b_spatial[0]), 128, axis=1)
    y_prompt = _mlp_prompt(x1, w_in1, lng_m, lnb_m, w_spatial[0], bsp, w_out1, lng1, lnb1)

    wsd = jnp.repeat(w_spatial[0, :, 0, 0], GW)[None]
    bsd = jnp.repeat(b_spatial[0, :, 0], GW)[None]
    y_sample, mlp_v = _sample_finish(gamma, x_sample, oin, oc, sg, gn_gain, w_out0, lng0, lnb0,
                                     w_in1, lng_m, lnb_m, wsd, bsd, w_out1, lng1, lnb1)

    return (y_prompt.reshape(batch, seq, D_MODEL),
            y_sample,
            ret_state_prompt[None],
            ret_state_sample[None],
            mlp_v)
```

```python
import functools

import jax
import jax.numpy as jnp
import numpy as np
from jax import lax
from jax.experimental import pallas as pl
from jax.experimental.pallas import tpu as pltpu
from jax.experimental.pallas import tpu_sc as plsc

F32 = jnp.float32
BF16 = jnp.bfloat16

D_MODEL = 1024
DEPTH = 2
PAST_LEN = 16384
HEADS = 4
DK = D_MODEL // HEADS
DV = 2 * DK
QKW = HEADS * DK
VW = HEADS * DV
CHUNK = 128
ROPE_BASE = 10000.0
ROPE_HALF = DK // 2
MLP_W = 2 * D_MODEL
GROUPS = 8
GW = MLP_W // GROUPS
ALPHA = (2 * DEPTH) ** 0.25
LN_EPS = 1e-5
SQRT_HALF = float(np.sqrt(0.5))

TILE = 256
RET_BLOCK = TILE
MLP_TILE = 256
RET_STAGE_ORDER = ("qk0", "qk1", "qk2", "qk3", "ret0", "ret1", "ret2", "ret3", "out",
                   "gate0", "gate1", "gate2", "gate3", "v0", "v1", "v2", "v3")
MLP_STAGE_ORDER = ("gate0", "v0", "mix0", "gate1", "v1", "mix1", "gate2", "v2", "out", "gate3", "v3")
VMEM_LIMIT = 60 * 1024 * 1024


def _dot(a, b):
    return jnp.dot(a, b, preferred_element_type=F32)


def _gelu(x):
    return 0.5 * x * (1.0 + lax.erf(x * SQRT_HALF))


def _silu(x):
    return x * jax.nn.sigmoid(x)


def _norm_rows(x):
    mu = jnp.mean(x, axis=-1, keepdims=True)
    d = x - mu
    var = jnp.mean(d * d, axis=-1, keepdims=True)
    return d * lax.rsqrt(var + LN_EPS)


def _rotate(h, cos, sin):
    x1 = h[:, :ROPE_HALF]
    x2 = h[:, ROPE_HALF:]
    return jnp.concatenate([x1 * cos - x2 * sin, x2 * cos + x1 * sin], axis=-1)


def _const_spec(*shape):
    return pl.BlockSpec(shape, lambda s: (0,) * len(shape), pipeline_mode=pl.Buffered(1))


def _ret_project_qk(h, xb, cos, sin, win_ref, q_scr, k_scr):
    qs = slice(h * DK, (h + 1) * DK)
    q_scr[:, qs] = _rotate(_dot(xb, win_ref[:, qs]), cos, sin).astype(BF16)
    ks = slice(QKW + h * DK, QKW + (h + 1) * DK)
    k_scr[:, qs] = _rotate(_dot(xb, win_ref[:, ks]), cos, sin) * (DK ** -0.5)


def _ret_project_gate(h, xb, win_ref, sg_scr):
    vs = slice(h * DV, (h + 1) * DV)
    sg_scr[:, vs] = _silu(_dot(xb, win_ref[:, 2 * QKW + VW + h * DV:2 * QKW + VW + (h + 1) * DV]))


def _ret_project_v(h, xb, win_ref, v_scr):
    vs = slice(h * DV, (h + 1) * DV)
    v_scr[:, vs] = _dot(xb, win_ref[:, 2 * QKW + h * DV:2 * QKW + (h + 1) * DV]).astype(BF16)


def _ret_head(h, q_scr, k_scr, v_scr, sg_scr, yin_scr, st_ref, gl_ref, decay_ref, cross_ref, kd_ref, gn_ref):
    qs = slice(h * DK, (h + 1) * DK)
    vs = slice(h * DV, (h + 1) * DV)
    for c in range(TILE // RET_BLOCK):
        rows = slice(c * RET_BLOCK, (c + 1) * RET_BLOCK)
        qc = q_scr[rows, qs]
        kf = k_scr[rows, qs]
        vc = v_scr[rows, vs]
        s = lax.dot_general(qc, kf.astype(BF16), (((1,), (1,)), ((), ())),
                            preferred_element_type=F32)
        s = s * decay_ref[h]
        state = st_ref[h]
        o = _dot(s.astype(BF16), vc) + _dot(qc, state.astype(BF16)) * jnp.tile(cross_ref[h], (1, DV // 128))
        kdec = (kf * jnp.tile(kd_ref[h], (1, DK // 128))).astype(BF16)
        st_ref[h] = gl_ref[h] * state + lax.dot_general(
            kdec, vc, (((0,), (0,)), ((), ())), preferred_element_type=F32)
        on = _norm_rows(o) * gn_ref[:, vs]
        yin_scr[rows, vs] = (sg_scr[rows, vs] * on).astype(BF16)


def _ret_out(x_ref, yin_scr, wout_ref, lng_ref, lnb_ref, y_ref):
    z = ALPHA * x_ref[...] + _dot(yin_scr[...], wout_ref[...])
    y_ref[...] = _norm_rows(z) * lng_ref[...] + lnb_ref[...]


def _ret_prompt_kernel(gl_ref, x_ref, xp_ref, cosr_ref, sinr_ref, cost_ref, sint_ref, win_ref,
                       decay_ref, cross_ref, kd_ref, gn_ref, wout_ref, lng_ref, lnb_ref, wa_ref, wb_ref,
                       y_ref, st_ref, wa16_ref, wb16_ref,
                       qa, ka, va, sga, qb, kb, vb, sgb, yin_scr, *, tiles, tiles_per_seq):
    s = pl.program_id(0)
    prev = jnp.maximum(s - 1, 0)
    tile_in_seq = jnp.minimum(s, tiles - 1) % tiles_per_seq

    wa16_ref[...] = wa_ref[...].astype(BF16)
    wb16_ref[...] = wb_ref[...].astype(BF16)

    @pl.when(prev % tiles_per_seq == 0)
    def _():
        st_ref[...] = jnp.zeros_like(st_ref)

    def step(cur, old):
        stages = {}
        if cur is not None:
            ct = cost_ref[pl.ds(tile_in_seq, 1), :]
            st = sint_ref[pl.ds(tile_in_seq, 1), :]
            cos = ct * cosr_ref[...] - st * sinr_ref[...]
            sin = st * cosr_ref[...] + ct * sinr_ref[...]
            xb = x_ref[...].astype(BF16)
            for h in range(HEADS):
                stages[f"qk{h}"] = functools.partial(_ret_project_qk, h, xb, cos, sin, win_ref, *cur[:2])
                stages[f"gate{h}"] = functools.partial(_ret_project_gate, h, xb, win_ref, cur[3])
                stages[f"v{h}"] = functools.partial(_ret_project_v, h, xb, win_ref, cur[2])
        if old is not None:
            for h in range(HEADS):
                stages[f"ret{h}"] = functools.partial(_ret_head, h, *old, yin_scr, st_ref, gl_ref, decay_ref,
                                                      cross_ref, kd_ref, gn_ref)
            stages["out"] = functools.partial(_ret_out, xp_ref, yin_scr, wout_ref, lng_ref, lnb_ref, y_ref)
        assert set(stages) <= set(RET_STAGE_ORDER)
        for name in RET_STAGE_ORDER:
            if name in stages:
                stages[name]()

    sets = ((qa, ka, va, sga), (qb, kb, vb, sgb))

    @pl.when(s == 0)
    def _():
        step(sets[0], None)

    for parity in range(2):
        @pl.when(jnp.logical_and(jnp.logical_and(s > 0, s < tiles), s % 2 == parity))
        def _():
            step(sets[parity], sets[1 - parity])

    @pl.when(s == tiles)
    def _():
        step(None, sets[(tiles - 1) % 2])


def _ret_prompt(x, rope, w_in, decay, cross, kd, gl, gn_gain, w_out, ln_g, ln_b, w_next_a, w_next_b, *, seq):
    cos_r, sin_r, cos_t, sin_t = rope
    n = x.shape[0]
    tiles = n // TILE
    tiles_per_seq = seq // TILE
    cur = lambda s: jnp.minimum(s, tiles - 1)
    prev = lambda s: jnp.maximum(s - 1, 0)
    slab_a = w_next_a.shape[0] // tiles
    slab_b = w_next_b.shape[0] // tiles
    assert slab_a * tiles == w_next_a.shape[0] and slab_b * tiles == w_next_b.shape[0]
    return pl.pallas_call(
        functools.partial(_ret_prompt_kernel, tiles=tiles, tiles_per_seq=tiles_per_seq),
        grid=(tiles + 1,),
        in_specs=[
            pl.BlockSpec(memory_space=pltpu.SMEM),
            pl.BlockSpec((TILE, D_MODEL), lambda s: (cur(s), 0)),
            pl.BlockSpec((TILE, D_MODEL), lambda s: (prev(s), 0)),
            _const_spec(TILE, ROPE_HALF),
            _const_spec(TILE, ROPE_HALF),
            _const_spec(tiles_per_seq, ROPE_HALF),
            _const_spec(tiles_per_seq, ROPE_HALF),
            _const_spec(D_MODEL, 2 * QKW + 2 * VW),
            _const_spec(HEADS, RET_BLOCK, RET_BLOCK),
            _const_spec(HEADS, RET_BLOCK, 128),
            _const_spec(HEADS, RET_BLOCK, 128),
            _const_spec(1, VW),
            _const_spec(VW, D_MODEL),
            _const_spec(1, D_MODEL),
            _const_spec(1, D_MODEL),
            pl.BlockSpec((slab_a, w_next_a.shape[1]), lambda s: (cur(s), 0)),
            pl.BlockSpec((slab_b, w_next_b.shape[1]), lambda s: (cur(s), 0)),
        ],
        out_specs=[
            pl.BlockSpec((TILE, D_MODEL), lambda s: (prev(s), 0)),
            pl.BlockSpec((None, HEADS, DK, DV), lambda s: (prev(s) // tiles_per_seq, 0, 0, 0)),
            pl.BlockSpec((slab_a, w_next_a.shape[1]), lambda s: (cur(s), 0)),
            pl.BlockSpec((slab_b, w_next_b.shape[1]), lambda s: (cur(s), 0)),
        ],
        out_shape=[
            jax.ShapeDtypeStruct((n, D_MODEL), F32),
            jax.ShapeDtypeStruct((n // seq, HEADS, DK, DV), F32),
            jax.ShapeDtypeStruct(w_next_a.shape, BF16),
            jax.ShapeDtypeStruct(w_next_b.shape, BF16),
        ],
        scratch_shapes=2 * [
            pltpu.VMEM((TILE, QKW), BF16),
            pltpu.VMEM((TILE, QKW), F32),
            pltpu.VMEM((TILE, VW), BF16),
            pltpu.VMEM((TILE, VW), F32),
        ] + [pltpu.VMEM((TILE, VW), BF16)],
        compiler_params=pltpu.CompilerParams(
            dimension_semantics=("arbitrary",),
            vmem_limit_bytes=VMEM_LIMIT),
        name="ret_prompt",
    )(gl, x, x, cos_r, sin_r, cos_t, sin_t, w_in, decay, cross, kd, gn_gain, w_out, ln_g, ln_b,
      w_next_a, w_next_b)


def _mlp_project_gate(j, xb, win_ref, p_scr):
    u = _dot(xb, win_ref[:, j * DV:(j + 1) * DV])
    g = _dot(xb, win_ref[:, 2 * MLP_W + j * DV:2 * MLP_W + (j + 1) * DV])
    p_scr[:, j * DV:(j + 1) * DV] = _gelu(u) * _silu(g)


def _mlp_project_v(j, xb, win_ref, gv_scr):
    gv_scr[:, j * DV:(j + 1) * DV] = _gelu(_dot(xb, win_ref[:, MLP_W + j * DV:MLP_W + (j + 1) * DV]))


def _causal_mix_weights(ws_ref):
    row = lax.broadcasted_iota(jnp.int32, (CHUNK, CHUNK), 0)
    col = lax.broadcasted_iota(jnp.int32, (CHUNK, CHUNK), 1)
    return [jnp.where(row >= col, ws_ref[g], 0.0).astype(BF16) for g in range(GROUPS)]


def _mlp_mix(c, ws, p_scr, gv_scr, yin_scr, lngm_ref, lnbm_ref, bsp_ref):
    rows = slice(c * CHUNK, (c + 1) * CHUNK)
    vn = (_norm_rows(gv_scr[rows, :]) * lngm_ref[...] + lnbm_ref[...]).astype(BF16)
    for g in range(GROUPS):
        gs = slice(g * GW, (g + 1) * GW)
        bias = jnp.tile(bsp_ref[:, g * 128:(g + 1) * 128], (1, GW // 128))
        mixed = _dot(ws[g], vn[:, gs]) + bias
        yin_scr[rows, gs] = (p_scr[rows, gs] * mixed).astype(BF16)


def _mlp_out(x_ref, yin_scr, wout_ref, lng_ref, lnb_ref, y_ref):
    z = ALPHA * x_ref[...] + _dot(yin_scr[...], wout_ref[...])
    y_ref[...] = _norm_rows(z) * lng_ref[...] + lnb_ref[...]


def _mlp_prompt_kernel(x_ref, xp_ref, win_ref, lngm_ref, lnbm_ref, ws_ref, bsp_ref, wout_ref,
                       lng_ref, lnb_ref,
                       y_ref,
                       pa, gva, pb, gvb, yin_scr, *, tiles):
    s = pl.program_id(0)

    def step(cur, old):
        stages = {}
        if cur is not None:
            xb = x_ref[...].astype(BF16)
            for j in range(MLP_W // DV):
                stages[f"gate{j}"] = functools.partial(_mlp_project_gate, j, xb, win_ref, cur[0])
                stages[f"v{j}"] = functools.partial(_mlp_project_v, j, xb, win_ref, cur[1])
        if old is not None:
            ws = _causal_mix_weights(ws_ref)
            for c in range(MLP_TILE // CHUNK):
                stages[f"mix{c}"] = functools.partial(_mlp_mix, c, ws, *old, yin_scr, lngm_ref, lnbm_ref,
                                                      bsp_ref)
            stages["out"] = functools.partial(_mlp_out, xp_ref, yin_scr, wout_ref, lng_ref, lnb_ref, y_ref)
        assert set(stages) <= set(MLP_STAGE_ORDER)
        for name in MLP_STAGE_ORDER:
            if name in stages:
                stages[name]()

    sets = ((pa, gva), (pb, gvb))

    @pl.when(s == 0)
    def _():
        step(sets[0], None)

    for parity in range(2):
        @pl.when(jnp.logical_and(jnp.logical_and(s > 0, s < tiles), s % 2 == parity))
        def _():
            step(sets[parity], sets[1 - parity])

    @pl.when(s == tiles)
    def _():
        step(None, sets[(tiles - 1) % 2])


def _mlp_prompt(x, w_in, lng_m, lnb_m, w_s, bsp, w_out, ln_g, ln_b):
    n = x.shape[0]
    tiles = n // MLP_TILE
    cur = lambda s: jnp.minimum(s, tiles - 1)
    prev = lambda s: jnp.maximum(s - 1, 0)
    return pl.pallas_call(
        functools.partial(_mlp_prompt_kernel, tiles=tiles),
        grid=(tiles + 1,),
        in_specs=[
            pl.BlockSpec((MLP_TILE, D_MODEL), lambda s: (cur(s), 0)),
            pl.BlockSpec((MLP_TILE, D_MODEL), lambda s: (prev(s), 0)),
            _const_spec(D_MODEL, 3 * MLP_W),
            _const_spec(1, MLP_W),
            _const_spec(1, MLP_W),
            _const_spec(GROUPS, CHUNK, CHUNK),
            _const_spec(CHUNK, GROUPS * 128),
            _const_spec(MLP_W, D_MODEL),
            _const_spec(1, D_MODEL),
            _const_spec(1, D_MODEL),
        ],
        out_specs=pl.BlockSpec((MLP_TILE, D_MODEL), lambda s: (prev(s), 0)),
        out_shape=jax.ShapeDtypeStruct((n, D_MODEL), F32),
        scratch_shapes=2 * [
            pltpu.VMEM((MLP_TILE, MLP_W), F32),
            pltpu.VMEM((MLP_TILE, MLP_W), F32),
        ] + [pltpu.VMEM((MLP_TILE, MLP_W), BF16)],
        compiler_params=pltpu.CompilerParams(
            dimension_semantics=("arbitrary",),
            vmem_limit_bytes=VMEM_LIMIT),
        name="mlp_prompt",
    )(x, x, w_in, lng_m, lnb_m, w_s, bsp, w_out, ln_g, ln_b)


SAMPLE_COLS = 1024


def _ret_sample_in_kernel(x_ref, cos_ref, sin_ref, win_ref, wo_ref,
                          q_ref, k_ref, v_ref, sg_ref, oin_ref, win16_ref, wo16_ref, xb_scr):
    i = pl.program_id(0)

    @pl.when(i == 0)
    def _():
        xb_scr[...] = x_ref[:, 0, :].astype(BF16)

    w16 = win_ref[...].astype(BF16)
    win16_ref[...] = w16
    wo16_ref[...] = wo_ref[...].astype(BF16)
    h = _dot(xb_scr[...], w16)
    cos = cos_ref[...]
    sin = sin_ref[...]
    q_steps = QKW // SAMPLE_COLS
    v_steps = VW // SAMPLE_COLS
    qk_heads = SAMPLE_COLS // DK
    v_heads = SAMPLE_COLS // DV

    @pl.when(i < q_steps)
    def _():
        for j in range(qk_heads):
            q_ref[i * qk_heads + j] = _rotate(h[:, j * DK:(j + 1) * DK], cos, sin)

    @pl.when(jnp.logical_and(i >= q_steps, i < 2 * q_steps))
    def _():
        for j in range(qk_heads):
            k_ref[(i - q_steps) * qk_heads + j] = _rotate(h[:, j * DK:(j + 1) * DK], cos, sin) * (DK ** -0.5)

    @pl.when(jnp.logical_and(i >= 2 * q_steps, i < 2 * q_steps + v_steps))
    def _():
        for j in range(v_heads):
            v_ref[(i - 2 * q_steps) * v_heads + j] = h[:, j * DV:(j + 1) * DV]

    @pl.when(i >= 2 * q_steps + v_steps)
    def _():
        for j in range(v_heads):
            sg_ref[(i - 2 * q_steps - v_steps) * v_heads + j] = _silu(h[:, j * DV:(j + 1) * DV])

    @pl.when(i == pl.num_programs(0) - 1)
    def _():
        for hd in range(HEADS):
            qk = jnp.sum(q_ref[hd].astype(BF16).astype(F32) * k_ref[hd].astype(BF16).astype(F32),
                         axis=-1, keepdims=True)
            oin_ref[hd] = qk.astype(BF16).astype(F32) * v_ref[hd].astype(BF16).astype(F32)


def _ret_sample_in(x, cos, sin, w_in, w_out):
    n = x.shape[0]
    steps = w_in.shape[1] // SAMPLE_COLS
    wo_rows = 512
    wo_steps = w_out.shape[0] // wo_rows
    assert QKW % SAMPLE_COLS == 0 and SAMPLE_COLS % DV == 0 and wo_steps <= steps
    whole = lambda *shape: pl.BlockSpec(shape, lambda i: (0,) * len(shape))
    wo_spec = pl.BlockSpec((wo_rows, w_out.shape[1]), lambda i: (jnp.minimum(i, wo_steps - 1), 0))
    return pl.pallas_call(
        _ret_sample_in_kernel,
        grid=(steps,),
        in_specs=[
            whole(n, 1, D_MODEL),
            whole(1, ROPE_HALF),
            whole(1, ROPE_HALF),
            pl.BlockSpec((D_MODEL, SAMPLE_COLS), lambda i: (0, i)),
            wo_spec,
        ],
        out_specs=[
            whole(HEADS, n, DK),
            whole(HEADS, n, DK),
            whole(HEADS, n, DV),
            whole(HEADS, n, DV),
            whole(HEADS, n, DV),
            pl.BlockSpec((D_MODEL, SAMPLE_COLS), lambda i: (0, i)),
            wo_spec,
        ],
        out_shape=[
            jax.ShapeDtypeStruct((HEADS, n, DK), F32),
            jax.ShapeDtypeStruct((HEADS, n, DK), F32),
            jax.ShapeDtypeStruct((HEADS, n, DV), F32),
            jax.ShapeDtypeStruct((HEADS, n, DV), F32),
            jax.ShapeDtypeStruct((HEADS, n, DV), F32),
            jax.ShapeDtypeStruct(w_in.shape, BF16),
            jax.ShapeDtypeStruct(w_out.shape, BF16),
        ],
        scratch_shapes=[pltpu.VMEM((n, D_MODEL), BF16)],
        compiler_params=pltpu.CompilerParams(
            dimension_semantics=("arbitrary",),
            vmem_limit_bytes=VMEM_LIMIT),
        name="ret_sample_in",
    )(x, cos, sin, w_in, w_out)


SC_LANES = 16
SC_ROWS = 32
SC_HALF = 256


def _state_sc_kernel(gam_hbm, s0_hbm, k_hbm, q_hbm, v_hbm, s1_hbm, o_hbm,
                     gam_v, v_v, o_v, k_v, q_v, s0_v, s1_v, *, pairs_per_worker, num_cores, n):
    wid = lax.axis_index("s") * num_cores + lax.axis_index("c")
    first = wid * pairs_per_worker
    head = first // n
    pltpu.sync_copy(gam_hbm, gam_v)
    pltpu.sync_copy(v_hbm.at[pl.ds(first, pairs_per_worker)], v_v)
    pltpu.sync_copy(k_hbm.at[pl.ds(first, pairs_per_worker)], k_v)
    pltpu.sync_copy(q_hbm.at[pl.ds(first, pairs_per_worker)], q_v)
    gam = gam_v[head, pl.ds(0, SC_LANES)]

    @pl.loop(0, pairs_per_worker)
    def _(pi):
        seq = first + pi - head * n
        pi_vec = jnp.full((SC_LANES,), pi, jnp.int32)
        for c in range(DV // SC_LANES):
            o_v[pi, pl.ds(c * SC_LANES, SC_LANES)] = jnp.zeros((SC_LANES,), F32)

        @pl.loop(0, DK // SC_ROWS)
        def _(blk):
            row0 = (seq * HEADS + head) * DK + blk * SC_ROWS
            pltpu.sync_copy(s0_hbm.at[pl.ds(row0, SC_ROWS)], s0_v)
            for half in range(DV // SC_HALF):
                cols = [half * SC_HALF + c * SC_LANES for c in range(SC_HALF // SC_LANES)]
                vregs = [v_v[pi, pl.ds(c, SC_LANES)] for c in cols]

                @pl.loop(0, SC_ROWS // 8)
                def _(grp):
                    acc = [None] * len(cols)
                    for j in range(8):
                        d = grp * 8 + j
                        d_vec = jnp.full((SC_LANES,), blk * SC_ROWS + d, jnp.int32)
                        kv = plsc.load_gather(k_v, [pi_vec, d_vec])
                        qv = plsc.load_gather(q_v, [pi_vec, d_vec])
                        for i, (c, vreg) in enumerate(zip(cols, vregs)):
                            s0 = s0_v[d, pl.ds(c, SC_LANES)]
                            s1_v[d, pl.ds(c, SC_LANES)] = gam * s0 + kv * vreg
                            acc[i] = qv * s0 if j == 0 else acc[i] + qv * s0
                    for c, a in zip(cols, acc):
                        plsc.addupdate(o_v.at[pi, pl.ds(c, SC_LANES)], a)
            pltpu.sync_copy(s1_v, s1_hbm.at[pl.ds(row0, SC_ROWS)])

    pltpu.sync_copy(o_v, o_hbm.at[pl.ds(first, pairs_per_worker)])


def _state_sc(gamma, s0, q, k, v):
    n = s0.shape[0]
    info = plsc.get_sparse_core_info()
    workers = info.num_cores * info.num_subcores
    pairs = n * HEADS
    pairs_per_worker = pairs // workers
    assert info.num_lanes == SC_LANES and pairs % workers == 0 and n % pairs_per_worker == 0
    gam_rows = jnp.broadcast_to(jnp.pad(gamma, (0, 8 - HEADS))[:, None], (8, 128))
    mesh = plsc.VectorSubcoreMesh(core_axis_name="c", subcore_axis_name="s")
    s1, o = pl.kernel(
        functools.partial(_state_sc_kernel, pairs_per_worker=pairs_per_worker, num_cores=info.num_cores, n=n),
        out_type=[
            jax.ShapeDtypeStruct((pairs * DK, DV), F32),
            jax.ShapeDtypeStruct((pairs, DV), F32),
        ],
        mesh=mesh,
        scratch_types=[
            pltpu.VMEM((8, 128), F32),
            pltpu.VMEM((pairs_per_worker, DV), F32),
            pltpu.VMEM((pairs_per_worker, DV), F32),
            pltpu.VMEM((pairs_per_worker, DK), F32),
            pltpu.VMEM((pairs_per_worker, DK), F32),
            pltpu.VMEM((SC_ROWS, DV), F32),
            pltpu.VMEM((SC_ROWS, DV), F32),
        ],
        compiler_params=pltpu.CompilerParams(use_tc_tiling_on_sc=True, needs_layout_passes=False),
        name="state_sc",
    )(gam_rows, s0.reshape(pairs * DK, DV), k.reshape(pairs, DK), q.reshape(pairs, DK), v.reshape(pairs, DV))
    return s1.reshape(s0.shape), o.reshape(HEADS, n, DV)


def _sample_finish_kernel(gam_ref, x_ref, oin_ref, oc_ref, sg_ref, gn_ref, wout0_ref, lng0_ref, lnb0_ref,
                          win_ref, lngm_ref, lnbm_ref, wsd_ref, bsd_ref, wout1_ref, lng1_ref, lnb1_ref,
                          y_ref, vout_ref, yin_scr):
    x = x_ref[:, 0, :]
    for h in range(HEADS):
        vs = slice(h * DV, (h + 1) * DV)
        o = oin_ref[h] + oc_ref[h] * gam_ref[h]
        yin_scr[:, vs] = (sg_ref[h] * (_norm_rows(o) * gn_ref[:, vs])).astype(BF16)
    z = ALPHA * x + _dot(yin_scr[...], wout0_ref[...])
    x1 = _norm_rows(z) * lng0_ref[...] + lnb0_ref[...]

    xb = x1.astype(BF16)
    u = _gelu(_dot(xb, win_ref[:, :MLP_W]))
    v = _gelu(_dot(xb, win_ref[:, MLP_W:2 * MLP_W]))
    g = _dot(xb, win_ref[:, 2 * MLP_W:])
    vn = _norm_rows(v) * lngm_ref[...] + lnbm_ref[...]
    vout_ref[0, :, 0, :] = vn
    mixed = wsd_ref[...].astype(BF16).astype(F32) * vn.astype(BF16).astype(F32) + bsd_ref[...]
    yin = (u * mixed * _silu(g)).astype(BF16)
    z1 = ALPHA * x1 + _dot(yin, wout1_ref[...])
    y_ref[:, 0, :] = _norm_rows(z1) * lng1_ref[...] + lnb1_ref[...]


def _sample_finish(gamma, x, oin, oc, sg, gn_gain, w_out0, lng0, lnb0,
                   w_in1, lng_m, lnb_m, wsd, bsd, w_out1, lng1, lnb1):
    n = x.shape[0]
    vmem = pl.BlockSpec(memory_space=pltpu.VMEM)
    return pl.pallas_call(
        _sample_finish_kernel,
        in_specs=[pl.BlockSpec(memory_space=pltpu.SMEM)] + [vmem] * 16,
        out_shape=[
            jax.ShapeDtypeStruct((n, 1, D_MODEL), F32),
            jax.ShapeDtypeStruct((1, n, 1, MLP_W), F32),
        ],
        scratch_shapes=[pltpu.VMEM((n, VW), BF16)],
        compiler_params=pltpu.CompilerParams(vmem_limit_bytes=VMEM_LIMIT),
        name="sample_finish",
    )(gamma, x, oin, oc, sg, gn_gain, w_out0, lng0, lnb0,
      w_in1, lng_m, lnb_m, wsd, bsd, w_out1, lng1, lnb1)


def _rope_tables(pos):
    inv = ROPE_BASE ** (-jnp.arange(ROPE_HALF, dtype=F32) / ROPE_HALF)
    ang = pos.astype(F32)[:, None] * inv[None, :]
    return jnp.cos(ang), jnp.sin(ang)


def _decay_tables():
    lg = jnp.log1p(-jnp.exp2(-5.0 - jnp.arange(HEADS, dtype=F32)))
    idx = jnp.arange(RET_BLOCK, dtype=F32)
    diff = idx[:, None] - idx[None, :]
    decay = jnp.where(diff >= 0, jnp.exp(jnp.maximum(diff, 0.0)[None] * lg[:, None, None]), 0.0)
    cross = jnp.exp((idx[:, None] + 1.0) * lg[None, :])
    kd = jnp.exp((RET_BLOCK - 1.0 - idx)[:, None] * lg[None, :])
    cross = jnp.broadcast_to(cross.T[:, :, None], (HEADS, RET_BLOCK, 128))
    kd = jnp.broadcast_to(kd.T[:, :, None], (HEADS, RET_BLOCK, 128))
    return decay, cross, kd, jnp.exp(RET_BLOCK * lg), jnp.exp(lg)


def kernel(x_prompt, x_sample, state_ret, ln_gain, ln_bias, w_in_ret, gn_gain_ret, w_out_ret,
           w_in_mlp, ln_gain_mlp, ln_bias_mlp, w_spatial, b_spatial, w_out_mlp):
    batch, seq, _ = x_prompt.shape
    n_dec = x_sample.shape[0]
    assert x_sample.shape[1] == 1 and seq % TILE == 0 and seq % MLP_TILE == 0

    lng0, lnb0 = ln_gain[0][None], ln_bias[0][None]
    lng1, lnb1 = ln_gain[1][None], ln_bias[1][None]
    gn_gain = gn_gain_ret[0][None]
    lng_m, lnb_m = ln_gain_mlp[0][None], ln_bias_mlp[0][None]

    rope = (*_rope_tables(jnp.arange(TILE, dtype=jnp.int32)),
            *_rope_tables(TILE * jnp.arange(seq // TILE, dtype=jnp.int32)))
    cos_s, sin_s = _rope_tables(PAST_LEN + jnp.arange(1, dtype=jnp.int32))
    decay, cross, kd, gamma_chunk, gamma = _decay_tables()

    q, k, v, sg, oin, w_in0, w_out0 = _ret_sample_in(x_sample, cos_s, sin_s, w_in_ret[0], w_out_ret[0])

    x1, ret_state_prompt, w_in1, w_out1 = _ret_prompt(
        x_prompt.reshape(batch * seq, D_MODEL), rope, w_in0, decay, cross, kd, gamma_chunk,
        gn_gain, w_out0, lng0, lnb0, w_in_mlp[0], w_out_mlp[0], seq=seq)
    bsp = jnp.repeat(jnp.transpose(b_spatial[0]), 128, axis=1)
    y_prompt = _mlp_prompt(x1, w_in1, lng_m, lnb_m, w_spatial[0], bsp, w_out1, lng1, lnb1)
    ret_state_sample, oc = _state_sc(gamma, state_ret[0], q, k, v)

    wsd = jnp.repeat(w_spatial[0, :, 0, 0], GW)[None]
    bsd = jnp.repeat(b_spatial[0, :, 0], GW)[None]
    y_sample, mlp_v = _sample_finish(gamma, x_sample, oin, oc, sg, gn_gain, w_out0, lng0, lnb0,
                                     w_in1, lng_m, lnb_m, wsd, bsd, w_out1, lng1, lnb1)

    return (y_prompt.reshape(batch, seq, D_MODEL),
            y_sample,
            ret_state_prompt[None],
            ret_state_sample[None],
            mlp_v)
```

```python
import functools

import jax
import jax.numpy as jnp
import numpy as np
from jax import lax
from jax.experimental import pallas as pl
from jax.experimental.pallas import tpu as pltpu
from jax.experimental.pallas import tpu_sc as plsc

F32 = jnp.float32
BF16 = jnp.bfloat16

D_MODEL = 1024
DEPTH = 2
PAST_LEN = 16384
HEADS = 4
DK = D_MODEL // HEADS
DV = 2 * DK
QKW = HEADS * DK
VW = HEADS * DV
CHUNK = 128
ROPE_BASE = 10000.0
ROPE_HALF = DK // 2
MLP_W = 2 * D_MODEL
GROUPS = 8
GW = MLP_W // GROUPS
ALPHA = (2 * DEPTH) ** 0.25
LN_EPS = 1e-5
SQRT_HALF = float(np.sqrt(0.5))

TILE = 256
RET_BLOCK = TILE
MLP_TILE = 256
RET_STAGE_ORDER = ("qk0", "qk1", "qk2", "qk3", "ret0", "ret1", "ret2", "ret3", "out",
                   "gate0", "gate1", "gate2", "gate3", "v0", "v1", "v2", "v3")
MLP_STAGE_ORDER = ("gate0", "v0", "mix0", "gate1", "v1", "mix1", "gate2", "v2", "out", "gate3", "v3")
VMEM_LIMIT = 60 * 1024 * 1024


def _dot(a, b):
    return jnp.dot(a, b, preferred_element_type=F32)


def _gelu(x):
    return 0.5 * x * (1.0 + lax.erf(x * SQRT_HALF))


def _silu(x):
    return x * jax.nn.sigmoid(x)


def _norm_rows(x):
    mu = jnp.mean(x, axis=-1, keepdims=True)
    d = x - mu
    var = jnp.mean(d * d, axis=-1, keepdims=True)
    return d * lax.rsqrt(var + LN_EPS)


def _rotate(h, cos, sin):
    x1 = h[:, :ROPE_HALF]
    x2 = h[:, ROPE_HALF:]
    return jnp.concatenate([x1 * cos - x2 * sin, x2 * cos + x1 * sin], axis=-1)


def _const_spec(*shape):
    return pl.BlockSpec(shape, lambda s: (0,) * len(shape), pipeline_mode=pl.Buffered(1))


def _ret_project_qk(h, xb, cos, sin, win_ref, q_scr, k_scr):
    qs = slice(h * DK, (h + 1) * DK)
    q_scr[:, qs] = _rotate(_dot(xb, win_ref[:, qs]), cos, sin).astype(BF16)
    ks = slice(QKW + h * DK, QKW + (h + 1) * DK)
    k_scr[:, qs] = _rotate(_dot(xb, win_ref[:, ks]), cos, sin) * (DK ** -0.5)


def _ret_project_gate(h, xb, win_ref, sg_scr):
    vs = slice(h * DV, (h + 1) * DV)
    sg_scr[:, vs] = _silu(_dot(xb, win_ref[:, 2 * QKW + VW + h * DV:2 * QKW + VW + (h + 1) * DV]))


def _ret_project_v(h, xb, win_ref, v_scr):
    vs = slice(h * DV, (h + 1) * DV)
    v_scr[:, vs] = _dot(xb, win_ref[:, 2 * QKW + h * DV:2 * QKW + (h + 1) * DV]).astype(BF16)


def _ret_head(h, q_scr, k_scr, v_scr, sg_scr, yin_scr, st_ref, gl_ref, decay_ref, cross_ref, kd_ref, gn_ref):
    qs = slice(h * DK, (h + 1) * DK)
    vs = slice(h * DV, (h + 1) * DV)
    for c in range(TILE // RET_BLOCK):
        rows = slice(c * RET_BLOCK, (c + 1) * RET_BLOCK)
        qc = q_scr[rows, qs]
        kf = k_scr[rows, qs]
        vc = v_scr[rows, vs]
        s = lax.dot_general(qc, kf.astype(BF16), (((1,), (1,)), ((), ())),
                            preferred_element_type=F32)
        s = s * decay_ref[h]
        state = st_ref[h]
        o = _dot(s.astype(BF16), vc) + _dot(qc, state.astype(BF16)) * jnp.tile(cross_ref[h], (1, DV // 128))
        kdec = (kf * jnp.tile(kd_ref[h], (1, DK // 128))).astype(BF16)
        st_ref[h] = gl_ref[h] * state + lax.dot_general(
            kdec, vc, (((0,), (0,)), ((), ())), preferred_element_type=F32)
        on = _norm_rows(o) * gn_ref[:, vs]
        yin_scr[rows, vs] = (sg_scr[rows, vs] * on).astype(BF16)


def _ret_out(x_ref, yin_scr, wout_ref, lng_ref, lnb_ref, y_ref):
    z = ALPHA * x_ref[...] + _dot(yin_scr[...], wout_ref[...])
    y_ref[...] = _norm_rows(z) * lng_ref[...] + lnb_ref[...]


def _ret_prompt_kernel(gl_ref, x_ref, xp_ref, cosr_ref, sinr_ref, cost_ref, sint_ref, win_ref,
                       decay_ref, cross_ref, kd_ref, gn_ref, wout_ref, lng_ref, lnb_ref, wa_ref, wb_ref,
                       y_ref, st_ref, wa16_ref, wb16_ref,
                       qa, ka, va, sga, qb, kb, vb, sgb, yin_scr, *, tiles, tiles_per_seq):
    s = pl.program_id(0)
    prev = jnp.maximum(s - 1, 0)
    tile_in_seq = jnp.minimum(s, tiles - 1) % tiles_per_seq

    wa16_ref[...] = wa_ref[...].astype(BF16)
    wb16_ref[...] = wb_ref[...].astype(BF16)

    @pl.when(prev % tiles_per_seq == 0)
    def _():
        st_ref[...] = jnp.zeros_like(st_ref)

    def step(cur, old):
        stages = {}
        if cur is not None:
            ct = cost_ref[pl.ds(tile_in_seq, 1), :]
            st = sint_ref[pl.ds(tile_in_seq, 1), :]
            cos = ct * cosr_ref[...] - st * sinr_ref[...]
            sin = st * cosr_ref[...] + ct * sinr_ref[...]
            xb = x_ref[...].astype(BF16)
            for h in range(HEADS):
                stages[f"qk{h}"] = functools.partial(_ret_project_qk, h, xb, cos, sin, win_ref, *cur[:2])
                stages[f"gate{h}"] = functools.partial(_ret_project_gate, h, xb, win_ref, cur[3])
                stages[f"v{h}"] = functools.partial(_ret_project_v, h, xb, win_ref, cur[2])
        if old is not None:
            for h in range(HEADS):
                stages[f"ret{h}"] = functools.partial(_ret_head, h, *old, yin_scr, st_ref, gl_ref, decay_ref,
                                                      cross_ref, kd_ref, gn_ref)
            stages["out"] = functools.partial(_ret_out, xp_ref, yin_scr, wout_ref, lng_ref, lnb_ref, y_ref)
        assert set(stages) <= set(RET_STAGE_ORDER)
        for name in RET_STAGE_ORDER:
            if name in stages:
                stages[name]()

    sets = ((qa, ka, va, sga), (qb, kb, vb, sgb))

    @pl.when(s == 0)
    def _():
        step(sets[0], None)

    for parity in range(2):
        @pl.when(jnp.logical_and(jnp.logical_and(s > 0, s < tiles), s % 2 == parity))
        def _():
            step(sets[parity], sets[1 - parity])

    @pl.when(s == tiles)
    def _():
        step(None, sets[(tiles - 1) % 2])


def _ret_prompt(x, rope, w_in, decay, cross, kd, gl, gn_gain, w_out, ln_g, ln_b, w_next_a, w_next_b, *, seq):
    cos_r, sin_r, cos_t, sin_t = rope
    n = x.shape[0]
    tiles = n // TILE
    tiles_per_seq = seq // TILE
    cur = lambda s: jnp.minimum(s, tiles - 1)
    prev = lambda s: jnp.maximum(s - 1, 0)
    slab_a = w_next_a.shape[0] // tiles
    slab_b = w_next_b.shape[0] // tiles
    assert slab_a * tiles == w_next_a.shape[0] and slab_b * tiles == w_next_b.shape[0]
    return pl.pallas_call(
        functools.partial(_ret_prompt_kernel, tiles=tiles, tiles_per_seq=tiles_per_seq),
        grid=(tiles + 1,),
        in_specs=[
            pl.BlockSpec(memory_space=pltpu.SMEM),
            pl.BlockSpec((TILE, D_MODEL), lambda s: (cur(s), 0)),
            pl.BlockSpec((TILE, D_MODEL), lambda s: (prev(s), 0)),
            _const_spec(TILE, ROPE_HALF),
            _const_spec(TILE, ROPE_HALF),
            _const_spec(tiles_per_seq, ROPE_HALF),
            _const_spec(tiles_per_seq, ROPE_HALF),
            _const_spec(D_MODEL, 2 * QKW + 2 * VW),
            _const_spec(HEADS, RET_BLOCK, RET_BLOCK),
            _const_spec(HEADS, RET_BLOCK, 128),
            _const_spec(HEADS, RET_BLOCK, 128),
            _const_spec(1, VW),
            _const_spec(VW, D_MODEL),
            _const_spec(1, D_MODEL),
            _const_spec(1, D_MODEL),
            pl.BlockSpec((slab_a, w_next_a.shape[1]), lambda s: (cur(s), 0)),
            pl.BlockSpec((slab_b, w_next_b.shape[1]), lambda s: (cur(s), 0)),
        ],
        out_specs=[
            pl.BlockSpec((TILE, D_MODEL), lambda s: (prev(s), 0)),
            pl.BlockSpec((None, HEADS, DK, DV), lambda s: (prev(s) // tiles_per_seq, 0, 0, 0)),
            pl.BlockSpec((slab_a, w_next_a.shape[1]), lambda s: (cur(s), 0)),
            pl.BlockSpec((slab_b, w_next_b.shape[1]), lambda s: (cur(s), 0)),
        ],
        out_shape=[
            jax.ShapeDtypeStruct((n, D_MODEL), F32),
            jax.ShapeDtypeStruct((n // seq, HEADS, DK, DV), F32),
            jax.ShapeDtypeStruct(w_next_a.shape, BF16),
            jax.ShapeDtypeStruct(w_next_b.shape, BF16),
        ],
        scratch_shapes=2 * [
            pltpu.VMEM((TILE, QKW), BF16),
            pltpu.VMEM((TILE, QKW), F32),
            pltpu.VMEM((TILE, VW), BF16),
            pltpu.VMEM((TILE, VW), F32),
        ] + [pltpu.VMEM((TILE, VW), BF16)],
        compiler_params=pltpu.CompilerParams(
            dimension_semantics=("arbitrary",),
            vmem_limit_bytes=VMEM_LIMIT),
        name="ret_prompt",
    )(gl, x, x, cos_r, sin_r, cos_t, sin_t, w_in, decay, cross, kd, gn_gain, w_out, ln_g, ln_b,
      w_next_a, w_next_b)


def _mlp_project_gate(j, xb, win_ref, p_scr):
    u = _dot(xb, win_ref[:, j * DV:(j + 1) * DV])
    g = _dot(xb, win_ref[:, 2 * MLP_W + j * DV:2 * MLP_W + (j + 1) * DV])
    p_scr[:, j * DV:(j + 1) * DV] = _gelu(u) * _silu(g)


def _mlp_project_v(j, xb, win_ref, gv_scr):
    gv_scr[:, j * DV:(j + 1) * DV] = _gelu(_dot(xb, win_ref[:, MLP_W + j * DV:MLP_W + (j + 1) * DV]))


def _causal_mix_weights(ws_ref):
    row = lax.broadcasted_iota(jnp.int32, (CHUNK, CHUNK), 0)
    col = lax.broadcasted_iota(jnp.int32, (CHUNK, CHUNK), 1)
    return [jnp.where(row >= col, ws_ref[g], 0.0).astype(BF16) for g in range(GROUPS)]


def _mlp_mix(c, ws, p_scr, gv_scr, yin_scr, lngm_ref, lnbm_ref, bsp_ref):
    rows = slice(c * CHUNK, (c + 1) * CHUNK)
    vn = (_norm_rows(gv_scr[rows, :]) * lngm_ref[...] + lnbm_ref[...]).astype(BF16)
    for g in range(GROUPS):
        gs = slice(g * GW, (g + 1) * GW)
        bias = jnp.tile(bsp_ref[:, g * 128:(g + 1) * 128], (1, GW // 128))
        mixed = _dot(ws[g], vn[:, gs]) + bias
        yin_scr[rows, gs] = (p_scr[rows, gs] * mixed).astype(BF16)


def _mlp_out(x_ref, yin_scr, wout_ref, lng_ref, lnb_ref, y_ref):
    z = ALPHA * x_ref[...] + _dot(yin_scr[...], wout_ref[...])
    y_ref[...] = _norm_rows(z) * lng_ref[...] + lnb_ref[...]


def _mlp_prompt_kernel(x_ref, xp_ref, win_ref, lngm_ref, lnbm_ref, ws_ref, bsp_ref, wout_ref,
                       lng_ref, lnb_ref,
                       y_ref,
                       pa, gva, pb, gvb, yin_scr, *, tiles):
    s = pl.program_id(0)

    def step(cur, old):
        stages = {}
        if cur is not None:
            xb = x_ref[...].astype(BF16)
            for j in range(MLP_W // DV):
                stages[f"gate{j}"] = functools.partial(_mlp_project_gate, j, xb, win_ref, cur[0])
                stages[f"v{j}"] = functools.partial(_mlp_project_v, j, xb, win_ref, cur[1])
        if old is not None:
            ws = _causal_mix_weights(ws_ref)
            for c in range(MLP_TILE // CHUNK):
                stages[f"mix{c}"] = functools.partial(_mlp_mix, c, ws, *old, yin_scr, lngm_ref, lnbm_ref,
                                                      bsp_ref)
            stages["out"] = functools.partial(_mlp_out, xp_ref, yin_scr, wout_ref, lng_ref, lnb_ref, y_ref)
        assert set(stages) <= set(MLP_STAGE_ORDER)
        for name in MLP_STAGE_ORDER:
            if name in stages:
                stages[name]()

    sets = ((pa, gva), (pb, gvb))

    @pl.when(s == 0)
    def _():
        step(sets[0], None)

    for parity in range(2):
        @pl.when(jnp.logical_and(jnp.logical_and(s > 0, s < tiles), s % 2 == parity))
        def _():
            step(sets[parity], sets[1 - parity])

    @pl.when(s == tiles)
    def _():
        step(None, sets[(tiles - 1) % 2])


def _mlp_prompt(x, w_in, lng_m, lnb_m, w_s, bsp, w_out, ln_g, ln_b):
    n = x.shape[0]
    tiles = n // MLP_TILE
    cur = lambda s: jnp.minimum(s, tiles - 1)
    prev = lambda s: jnp.maximum(s - 1, 0)
    return pl.pallas_call(
        functools.partial(_mlp_prompt_kernel, tiles=tiles),
        grid=(tiles + 1,),
        in_specs=[
            pl.BlockSpec((MLP_TILE, D_MODEL), lambda s: (cur(s), 0)),
            pl.BlockSpec((MLP_TILE, D_MODEL), lambda s: (prev(s), 0)),
            _const_spec(D_MODEL, 3 * MLP_W),
            _const_spec(1, MLP_W),
            _const_spec(1, MLP_W),
            _const_spec(GROUPS, CHUNK, CHUNK),
            _const_spec(CHUNK, GROUPS * 128),
            _const_spec(MLP_W, D_MODEL),
            _const_spec(1, D_MODEL),
            _const_spec(1, D_MODEL),
        ],
        out_specs=pl.BlockSpec((MLP_TILE, D_MODEL), lambda s: (prev(s), 0)),
        out_shape=jax.ShapeDtypeStruct((n, D_MODEL), F32),
        scratch_shapes=2 * [
            pltpu.VMEM((MLP_TILE, MLP_W), F32),
            pltpu.VMEM((MLP_TILE, MLP_W), F32),
        ] + [pltpu.VMEM((MLP_TILE, MLP_W), BF16)],
        compiler_params=pltpu.CompilerParams(
            dimension_semantics=("arbitrary",),
            vmem_limit_bytes=VMEM_LIMIT),
        name="mlp_prompt",
    )(x, x, w_in, lng_m, lnb_m, w_s, bsp, w_out, ln_g, ln_b)


SAMPLE_COLS = 1024


def _ret_sample_in_kernel(x_ref, cos_ref, sin_ref, win_ref, wo_ref,
                          q_ref, k_ref, v_ref, sg_ref, oin_ref, win16_ref, wo16_ref, xb_scr):
    i = pl.program_id(0)

    @pl.when(i == 0)
    def _():
        xb_scr[...] = x_ref[:, 0, :].astype(BF16)

    w16 = win_ref[...].astype(BF16)
    win16_ref[...] = w16
    wo16_ref[...] = wo_ref[...].astype(BF16)
    h = _dot(xb_scr[...], w16)
    cos = cos_ref[...]
    sin = sin_ref[...]
    q_steps = QKW // SAMPLE_COLS
    v_steps = VW // SAMPLE_COLS
    qk_heads = SAMPLE_COLS // DK
    v_heads = SAMPLE_COLS // DV

    @pl.when(i < q_steps)
    def _():
        for j in range(qk_heads):
            q_ref[i * qk_heads + j] = _rotate(h[:, j * DK:(j + 1) * DK], cos, sin)

    @pl.when(jnp.logical_and(i >= q_steps, i < 2 * q_steps))
    def _():
        for j in range(qk_heads):
            k_ref[(i - q_steps) * qk_heads + j] = _rotate(h[:, j * DK:(j + 1) * DK], cos, sin) * (DK ** -0.5)

    @pl.when(jnp.logical_and(i >= 2 * q_steps, i < 2 * q_steps + v_steps))
    def _():
        for j in range(v_heads):
            v_ref[(i - 2 * q_steps) * v_heads + j] = h[:, j * DV:(j + 1) * DV]

    @pl.when(i >= 2 * q_steps + v_steps)
    def _():
        for j in range(v_heads):
            sg_ref[(i - 2 * q_steps - v_steps) * v_heads + j] = _silu(h[:, j * DV:(j + 1) * DV])

    @pl.when(i == pl.num_programs(0) - 1)
    def _():
        for hd in range(HEADS):
            qk = jnp.sum(q_ref[hd].astype(BF16).astype(F32) * k_ref[hd].astype(BF16).astype(F32),
                         axis=-1, keepdims=True)
            oin_ref[hd] = qk.astype(BF16).astype(F32) * v_ref[hd].astype(BF16).astype(F32)


def _ret_sample_in(x, cos, sin, w_in, w_out):
    n = x.shape[0]
    steps = w_in.shape[1] // SAMPLE_COLS
    wo_rows = 512
    wo_steps = w_out.shape[0] // wo_rows
    assert QKW % SAMPLE_COLS == 0 and SAMPLE_COLS % DV == 0 and wo_steps <= steps
    whole = lambda *shape: pl.BlockSpec(shape, lambda i: (0,) * len(shape))
    wo_spec = pl.BlockSpec((wo_rows, w_out.shape[1]), lambda i: (jnp.minimum(i, wo_steps - 1), 0))
    return pl.pallas_call(
        _ret_sample_in_kernel,
        grid=(steps,),
        in_specs=[
            whole(n, 1, D_MODEL),
            whole(1, ROPE_HALF),
            whole(1, ROPE_HALF),
            pl.BlockSpec((D_MODEL, SAMPLE_COLS), lambda i: (0, i)),
            wo_spec,
        ],
        out_specs=[
            whole(HEADS, n, DK),
            whole(HEADS, n, DK),
            whole(HEADS, n, DV),
            whole(HEADS, n, DV),
            whole(HEADS, n, DV),
            pl.BlockSpec((D_MODEL, SAMPLE_COLS), lambda i: (0, i)),
            wo_spec,
        ],
        out_shape=[
            jax.ShapeDtypeStruct((HEADS, n, DK), F32),
            jax.ShapeDtypeStruct((HEADS, n, DK), F32),
            jax.ShapeDtypeStruct((HEADS, n, DV), F32),
            jax.ShapeDtypeStruct((HEADS, n, DV), F32),
            jax.ShapeDtypeStruct((HEADS, n, DV), F32),
            jax.ShapeDtypeStruct(w_in.shape, BF16),
            jax.ShapeDtypeStruct(w_out.shape, BF16),
        ],
        scratch_shapes=[pltpu.VMEM((n, D_MODEL), BF16)],
        compiler_params=pltpu.CompilerParams(
            dimension_semantics=("arbitrary",),
            vmem_limit_bytes=VMEM_LIMIT),
        name="ret_sample_in",
    )(x, cos, sin, w_in, w_out)


SC_LANES = 16
SC_ROWS = 32
SC_HALF = 256


def _state_sc_kernel(gam_hbm, s0_hbm, k_hbm, q_hbm, v_hbm, s1_hbm, o_hbm,
                     gam_v, v_v, o_v, k_v, q_v, s0_v, s1_v, *, pairs_per_worker, num_cores, n):
    wid = lax.axis_index("s") * num_cores + lax.axis_index("c")
    first = wid * pairs_per_worker
    head = first // n
    pltpu.sync_copy(gam_hbm, gam_v)
    pltpu.sync_copy(v_hbm.at[pl.ds(first, pairs_per_worker)], v_v)
    pltpu.sync_copy(k_hbm.at[pl.ds(first, pairs_per_worker)], k_v)
    pltpu.sync_copy(q_hbm.at[pl.ds(first, pairs_per_worker)], q_v)
    gam = gam_v[head, pl.ds(0, SC_LANES)]

    @pl.loop(0, pairs_per_worker)
    def _(pi):
        seq = first + pi - head * n
        pi_vec = jnp.full((SC_LANES,), pi, jnp.int32)
        for c in range(DV // SC_LANES):
            o_v[pi, pl.ds(c * SC_LANES, SC_LANES)] = jnp.zeros((SC_LANES,), F32)

        @pl.loop(0, DK // SC_ROWS)
        def _(blk):
            row0 = (seq * HEADS + head) * DK + blk * SC_ROWS
            pltpu.sync_copy(s0_hbm.at[pl.ds(row0, SC_ROWS)], s0_v)
            for half in range(DV // SC_HALF):
                cols = [half * SC_HALF + c * SC_LANES for c in range(SC_HALF // SC_LANES)]
                vregs = [v_v[pi, pl.ds(c, SC_LANES)] for c in cols]

                @pl.loop(0, SC_ROWS // 8)
                def _(grp):
                    acc = [None] * len(cols)
                    for j in range(8):
                        d = grp * 8 + j
                        d_vec = jnp.full((SC_LANES,), blk * SC_ROWS + d, jnp.int32)
                        kv = plsc.load_gather(k_v, [pi_vec, d_vec])
                        qv = plsc.load_gather(q_v, [pi_vec, d_vec])
                        for i, (c, vreg) in enumerate(zip(cols, vregs)):
                            s0 = s0_v[d, pl.ds(c, SC_LANES)]
                            s1_v[d, pl.ds(c, SC_LANES)] = gam * s0 + kv * vreg
                            acc[i] = qv * s0 if j == 0 else acc[i] + qv * s0
                    for c, a in zip(cols, acc):
                        plsc.addupdate(o_v.at[pi, pl.ds(c, SC_LANES)], a)
            pltpu.sync_copy(s1_v, s1_hbm.at[pl.ds(row0, SC_ROWS)])

    pltpu.sync_copy(o_v, o_hbm.at[pl.ds(first, pairs_per_worker)])


def _state_sc(gamma, s0, q, k, v):
    n = s0.shape[0]
    info = plsc.get_sparse_core_info()
    workers = info.num_cores * info.num_subcores
    pairs = n * HEADS
    pairs_per_worker = pairs // workers
    assert info.num_lanes == SC_LANES and pairs % workers == 0 and n % pairs_per_worker == 0
    gam_rows = jnp.broadcast_to(jnp.pad(gamma, (0, 8 - HEADS))[:, None], (8, 128))
    mesh = plsc.VectorSubcoreMesh(core_axis_name="c", subcore_axis_name="s")
    s1, o = pl.kernel(
        functools.partial(_state_sc_kernel, pairs_per_worker=pairs_per_worker, num_cores=info.num_cores, n=n),
        out_type=[
            jax.ShapeDtypeStruct((pairs * DK, DV), F32),
            jax.ShapeDtypeStruct((pairs, DV), F32),
        ],
        mesh=mesh,
        scratch_types=[
            pltpu.VMEM((8, 128), F32),
            pltpu.VMEM((pairs_per_worker, DV), F32),
            pltpu.VMEM((pairs_per_worker, DV), F32),
            pltpu.VMEM((pairs_per_worker, DK), F32),
            pltpu.VMEM((pairs_per_worker, DK), F32),
            pltpu.VMEM((SC_ROWS, DV), F32),
            pltpu.VMEM((SC_ROWS, DV), F32),
        ],
        compiler_params=pltpu.CompilerParams(use_tc_tiling_on_sc=True, needs_layout_passes=False),
        name="state_sc",
    )(gam_rows, s0.reshape(pairs * DK, DV), k.reshape(pairs, DK), q.reshape(pairs, DK), v.reshape(pairs, DV))
    return s1.reshape(s0.shape), o.reshape(HEADS, n, DV)


FINISH_COLS = 1024


def _sample_finish_kernel(gam_ref, x_ref, oin_ref, oc_ref, sg_ref, gn_ref, wout0_ref, lng0_ref, lnb0_ref,
                          win_ref, lngm_ref, lnbm_ref, wsd_ref, bsd_ref, wout1_ref, lng1_ref, lnb1_ref,
                          y_ref, vout_ref, yin_scr, x1_scr, xb_scr, act_scr):
    i = pl.program_id(0)
    steps = pl.num_programs(0)

    @pl.when(i == 0)
    def _():
        x = x_ref[:, 0, :]
        for h in range(HEADS):
            vs = slice(h * DV, (h + 1) * DV)
            o = oin_ref[h] + oc_ref[h] * gam_ref[h]
            yin_scr[:, vs] = (sg_ref[h] * (_norm_rows(o) * gn_ref[:, vs])).astype(BF16)
        z = ALPHA * x + _dot(yin_scr[...], wout0_ref[...])
        x1 = _norm_rows(z) * lng0_ref[...] + lnb0_ref[...]
        x1_scr[...] = x1
        xb_scr[...] = x1.astype(BF16)

    h = _dot(xb_scr[...], win_ref[...])
    gate_start = 2 * MLP_W // FINISH_COLS
    for k in range(3 * MLP_W // FINISH_COLS):
        @pl.when(i == k)
        def _():
            cols = slice(k * FINISH_COLS, (k + 1) * FINISH_COLS)
            act_scr[:, cols] = _silu(h) if k >= gate_start else _gelu(h)

    @pl.when(i == steps - 1)
    def _():
        u = act_scr[:, :MLP_W]
        v = act_scr[:, MLP_W:2 * MLP_W]
        sg = act_scr[:, 2 * MLP_W:]
        vn = _norm_rows(v) * lngm_ref[...] + lnbm_ref[...]
        vout_ref[0, :, 0, :] = vn
        mixed = wsd_ref[...].astype(BF16).astype(F32) * vn.astype(BF16).astype(F32) + bsd_ref[...]
        yin = (u * mixed * sg).astype(BF16)
        z1 = ALPHA * x1_scr[...] + _dot(yin, wout1_ref[...])
        y_ref[:, 0, :] = _norm_rows(z1) * lng1_ref[...] + lnb1_ref[...]


def _sample_finish(gamma, x, oin, oc, sg, gn_gain, w_out0, lng0, lnb0,
                   w_in1, lng_m, lnb_m, wsd, bsd, w_out1, lng1, lnb1):
    n = x.shape[0]
    steps = w_in1.shape[1] // FINISH_COLS
    assert MLP_W % FINISH_COLS == 0
    whole = lambda a: pl.BlockSpec(a.shape, lambda i: (0,) * a.ndim, pipeline_mode=pl.Buffered(1))
    streamed = pl.BlockSpec((D_MODEL, FINISH_COLS), lambda i: (0, i))
    before = (x, oin, oc, sg, gn_gain, w_out0, lng0, lnb0)
    after = (lng_m, lnb_m, wsd, bsd, w_out1, lng1, lnb1)
    in_specs = ([pl.BlockSpec(memory_space=pltpu.SMEM)] + [whole(a) for a in before] + [streamed]
                + [whole(a) for a in after])
    return pl.pallas_call(
        _sample_finish_kernel,
        grid=(steps,),
        in_specs=in_specs,
        out_specs=[
            pl.BlockSpec((n, 1, D_MODEL), lambda i: (0, 0, 0)),
            pl.BlockSpec((1, n, 1, MLP_W), lambda i: (0, 0, 0, 0)),
        ],
        out_shape=[
            jax.ShapeDtypeStruct((n, 1, D_MODEL), F32),
            jax.ShapeDtypeStruct((1, n, 1, MLP_W), F32),
        ],
        scratch_shapes=[
            pltpu.VMEM((n, VW), BF16),
            pltpu.VMEM((n, D_MODEL), F32),
            pltpu.VMEM((n, D_MODEL), BF16),
            pltpu.VMEM((n, 3 * MLP_W), F32),
        ],
        compiler_params=pltpu.CompilerParams(
            dimension_semantics=("arbitrary",),
            vmem_limit_bytes=VMEM_LIMIT),
        name="sample_finish",
    )(gamma, *before, w_in1, *after)


def _rope_tables(pos):
    inv = ROPE_BASE ** (-jnp.arange(ROPE_HALF, dtype=F32) / ROPE_HALF)
    ang = pos.astype(F32)[:, None] * inv[None, :]
    return jnp.cos(ang), jnp.sin(ang)


def _decay_tables():
    lg = jnp.log1p(-jnp.exp2(-5.0 - jnp.arange(HEADS, dtype=F32)))
    idx = jnp.arange(RET_BLOCK, dtype=F32)
    diff = idx[:, None] - idx[None, :]
    decay = jnp.where(diff >= 0, jnp.exp(jnp.maximum(diff, 0.0)[None] * lg[:, None, None]), 0.0)
    cross = jnp.exp((idx[:, None] + 1.0) * lg[None, :])
    kd = jnp.exp((RET_BLOCK - 1.0 - idx)[:, None] * lg[None, :])
    cross = jnp.broadcast_to(cross.T[:, :, None], (HEADS, RET_BLOCK, 128))
    kd = jnp.broadcast_to(kd.T[:, :, None], (HEADS, RET_BLOCK, 128))
    return decay, cross, kd, jnp.exp(RET_BLOCK * lg), jnp.exp(lg)


def kernel(x_prompt, x_sample, state_ret, ln_gain, ln_bias, w_in_ret, gn_gain_ret, w_out_ret,
           w_in_mlp, ln_gain_mlp, ln_bias_mlp, w_spatial, b_spatial, w_out_mlp):
    batch, seq, _ = x_prompt.shape
    n_dec = x_sample.shape[0]
    assert x_sample.shape[1] == 1 and seq % TILE == 0 and seq % MLP_TILE == 0

    lng0, lnb0 = ln_gain[0][None], ln_bias[0][None]
    lng1, lnb1 = ln_gain[1][None], ln_bias[1][None]
    gn_gain = gn_gain_ret[0][None]
    lng_m, lnb_m = ln_gain_mlp[0][None], ln_bias_mlp[0][None]

    rope = (*_rope_tables(jnp.arange(TILE, dtype=jnp.int32)),
            *_rope_tables(TILE * jnp.arange(seq // TILE, dtype=jnp.int32)))
    cos_s, sin_s = _rope_tables(PAST_LEN + jnp.arange(1, dtype=jnp.int32))
    decay, cross, kd, gamma_chunk, gamma = _decay_tables()

    q, k, v, sg, oin, w_in0, w_out0 = _ret_sample_in(x_sample, cos_s, sin_s, w_in_ret[0], w_out_ret[0])

    x1, ret_state_prompt, w_in1, w_out1 = _ret_prompt(
        x_prompt.reshape(batch * seq, D_MODEL), rope, w_in0, decay, cross, kd, gamma_chunk,
        gn_gain, w_out0, lng0, lnb0, w_in_mlp[0], w_out_mlp[0], seq=seq)
    bsp = jnp.repeat(jnp.transpose(b_spatial[0]), 128, axis=1)
    y_prompt = _mlp_prompt(x1, w_in1, lng_m, lnb_m, w_spatial[0], bsp, w_out1, lng1, lnb1)
    ret_state_sample, oc = _state_sc(gamma, state_ret[0], q, k, v)

    wsd = jnp.repeat(w_spatial[0, :, 0, 0], GW)[None]
    bsd = jnp.repeat(b_spatial[0, :, 0], GW)[None]
    y_sample, mlp_v = _sample_finish(gamma, x_sample, oin, oc, sg, gn_gain, w_out0, lng0, lnb0,
                                     w_in1, lng_m, lnb_m, wsd, bsd, w_out1, lng1, lnb1)

    return (y_prompt.reshape(batch, seq, D_MODEL),
            y_sample,
            ret_state_prompt[None],
            ret_state_sample[None],
            mlp_v)
```

```python
import functools

import jax
import jax.numpy as jnp
import numpy as np
from jax import lax
from jax.experimental import pallas as pl
from jax.experimental.pallas import tpu as pltpu
from jax.experimental.pallas import tpu_sc as plsc

F32 = jnp.float32
BF16 = jnp.bfloat16

D_MODEL = 1024
DEPTH = 2
PAST_LEN = 16384
HEADS = 4
DK = D_MODEL // HEADS
DV = 2 * DK
QKW = HEADS * DK
VW = HEADS * DV
CHUNK = 128
ROPE_BASE = 10000.0
ROPE_HALF = DK // 2
MLP_W = 2 * D_MODEL
GROUPS = 8
GW = MLP_W // GROUPS
ALPHA = (2 * DEPTH) ** 0.25
LN_EPS = 1e-5
SQRT_HALF = float(np.sqrt(0.5))

TILE = 256
RET_BLOCK = TILE
MLP_TILE = 256
RET_STAGE_ORDER = ("qk0", "qk1", "qk2", "qk3", "ret0", "ret1", "ret2", "ret3", "out",
                   "gate0", "gate1", "gate2", "gate3", "v0", "v1", "v2", "v3")
MLP_STAGE_ORDER = ("gate0", "v0", "mix0", "gate1", "v1", "mix1", "gate2", "v2", "out", "gate3", "v3")
VMEM_LIMIT = 60 * 1024 * 1024


def _dot(a, b):
    return jnp.dot(a, b, preferred_element_type=F32)


def _gelu(x):
    return 0.5 * x * (1.0 + lax.erf(x * SQRT_HALF))


def _silu(x):
    return x * jax.nn.sigmoid(x)


def _norm_rows(x):
    mu = jnp.mean(x, axis=-1, keepdims=True)
    d = x - mu
    var = jnp.mean(d * d, axis=-1, keepdims=True)
    return d * lax.rsqrt(var + LN_EPS)


def _rotate(h, cos, sin):
    x1 = h[:, :ROPE_HALF]
    x2 = h[:, ROPE_HALF:]
    return jnp.concatenate([x1 * cos - x2 * sin, x2 * cos + x1 * sin], axis=-1)


def _const_spec(*shape):
    return pl.BlockSpec(shape, lambda s: (0,) * len(shape), pipeline_mode=pl.Buffered(1))


def _ret_project_qk(h, xb, cos, sin, win_ref, q_scr, k_scr):
    qs = slice(h * DK, (h + 1) * DK)
    q_scr[:, qs] = _rotate(_dot(xb, win_ref[:, qs]), cos, sin).astype(BF16)
    ks = slice(QKW + h * DK, QKW + (h + 1) * DK)
    k_scr[qs, :] = (_rotate(_dot(xb, win_ref[:, ks]), cos, sin) * (DK ** -0.5)).T


def _ret_project_gate(h, xb, win_ref, sg_scr):
    vs = slice(h * DV, (h + 1) * DV)
    sg_scr[:, vs] = _silu(_dot(xb, win_ref[:, 2 * QKW + VW + h * DV:2 * QKW + VW + (h + 1) * DV]))


def _ret_project_v(h, xb, win_ref, v_scr):
    vs = slice(h * DV, (h + 1) * DV)
    v_scr[:, vs] = _dot(xb, win_ref[:, 2 * QKW + h * DV:2 * QKW + (h + 1) * DV]).astype(BF16)


def _ret_head(h, q_scr, k_scr, v_scr, sg_scr, yin_scr, st_ref, gl_ref, decay_ref, cross_ref, kd_ref, gn_ref):
    qs = slice(h * DK, (h + 1) * DK)
    vs = slice(h * DV, (h + 1) * DV)
    for c in range(TILE // RET_BLOCK):
        rows = slice(c * RET_BLOCK, (c + 1) * RET_BLOCK)
        qc = q_scr[rows, qs]
        kt = k_scr[qs, rows]
        vc = v_scr[rows, vs]
        s = _dot(qc, kt.astype(BF16)) * decay_ref[h]
        state = st_ref[h]
        o = _dot(s.astype(BF16), vc) + _dot(qc, state.astype(BF16)) * jnp.tile(cross_ref[h], (1, DV // 128))
        kdec = (kt * kd_ref[h]).astype(BF16)
        st_ref[h] = gl_ref[h] * state + _dot(kdec, vc)
        on = _norm_rows(o) * gn_ref[:, vs]
        yin_scr[rows, vs] = (sg_scr[rows, vs] * on).astype(BF16)


def _ret_out(x_ref, yin_scr, wout_ref, lng_ref, lnb_ref, y_ref):
    z = ALPHA * x_ref[...] + _dot(yin_scr[...], wout_ref[...])
    y_ref[...] = _norm_rows(z) * lng_ref[...] + lnb_ref[...]


def _ret_prompt_kernel(gl_ref, x_ref, xp_ref, cosr_ref, sinr_ref, cost_ref, sint_ref, win_ref,
                       decay_ref, cross_ref, kd_ref, gn_ref, wout_ref, lng_ref, lnb_ref, wa_ref, wb_ref,
                       y_ref, st_ref, wa16_ref, wb16_ref,
                       qa, ka, va, sga, qb, kb, vb, sgb, yin_scr, *, tiles, tiles_per_seq):
    s = pl.program_id(0)
    prev = jnp.maximum(s - 1, 0)
    tile_in_seq = jnp.minimum(s, tiles - 1) % tiles_per_seq

    wa16_ref[...] = wa_ref[...].astype(BF16)
    wb16_ref[...] = wb_ref[...].astype(BF16)

    @pl.when(prev % tiles_per_seq == 0)
    def _():
        st_ref[...] = jnp.zeros_like(st_ref)

    def step(cur, old):
        stages = {}
        if cur is not None:
            ct = cost_ref[pl.ds(tile_in_seq, 1), :]
            st = sint_ref[pl.ds(tile_in_seq, 1), :]
            cos = ct * cosr_ref[...] - st * sinr_ref[...]
            sin = st * cosr_ref[...] + ct * sinr_ref[...]
            xb = x_ref[...].astype(BF16)
            for h in range(HEADS):
                stages[f"qk{h}"] = functools.partial(_ret_project_qk, h, xb, cos, sin, win_ref, *cur[:2])
                stages[f"gate{h}"] = functools.partial(_ret_project_gate, h, xb, win_ref, cur[3])
                stages[f"v{h}"] = functools.partial(_ret_project_v, h, xb, win_ref, cur[2])
        if old is not None:
            for h in range(HEADS):
                stages[f"ret{h}"] = functools.partial(_ret_head, h, *old, yin_scr, st_ref, gl_ref, decay_ref,
                                                      cross_ref, kd_ref, gn_ref)
            stages["out"] = functools.partial(_ret_out, xp_ref, yin_scr, wout_ref, lng_ref, lnb_ref, y_ref)
        assert set(stages) <= set(RET_STAGE_ORDER)
        for name in RET_STAGE_ORDER:
            if name in stages:
                stages[name]()

    sets = ((qa, ka, va, sga), (qb, kb, vb, sgb))

    @pl.when(s == 0)
    def _():
        step(sets[0], None)

    for parity in range(2):
        @pl.when(jnp.logical_and(jnp.logical_and(s > 0, s < tiles), s % 2 == parity))
        def _():
            step(sets[parity], sets[1 - parity])

    @pl.when(s == tiles)
    def _():
        step(None, sets[(tiles - 1) % 2])


def _ret_prompt(x, rope, w_in, decay, cross, kd, gl, gn_gain, w_out, ln_g, ln_b, w_next_a, w_next_b, *, seq):
    cos_r, sin_r, cos_t, sin_t = rope
    n = x.shape[0]
    tiles = n // TILE
    tiles_per_seq = seq // TILE
    cur = lambda s: jnp.minimum(s, tiles - 1)
    prev = lambda s: jnp.maximum(s - 1, 0)
    slab_a = w_next_a.shape[0] // tiles
    slab_b = w_next_b.shape[0] // tiles
    assert slab_a * tiles == w_next_a.shape[0] and slab_b * tiles == w_next_b.shape[0]
    return pl.pallas_call(
        functools.partial(_ret_prompt_kernel, tiles=tiles, tiles_per_seq=tiles_per_seq),
        grid=(tiles + 1,),
        in_specs=[
            pl.BlockSpec(memory_space=pltpu.SMEM),
            pl.BlockSpec((TILE, D_MODEL), lambda s: (cur(s), 0)),
            pl.BlockSpec((TILE, D_MODEL), lambda s: (prev(s), 0)),
            _const_spec(TILE, ROPE_HALF),
            _const_spec(TILE, ROPE_HALF),
            _const_spec(tiles_per_seq, ROPE_HALF),
            _const_spec(tiles_per_seq, ROPE_HALF),
            _const_spec(D_MODEL, 2 * QKW + 2 * VW),
            _const_spec(HEADS, RET_BLOCK, RET_BLOCK),
            _const_spec(HEADS, RET_BLOCK, 128),
            _const_spec(HEADS, 1, RET_BLOCK),
            _const_spec(1, VW),
            _const_spec(VW, D_MODEL),
            _const_spec(1, D_MODEL),
            _const_spec(1, D_MODEL),
            pl.BlockSpec((slab_a, w_next_a.shape[1]), lambda s: (cur(s), 0)),
            pl.BlockSpec((slab_b, w_next_b.shape[1]), lambda s: (cur(s), 0)),
        ],
        out_specs=[
            pl.BlockSpec((TILE, D_MODEL), lambda s: (prev(s), 0)),
            pl.BlockSpec((None, HEADS, DK, DV), lambda s: (prev(s) // tiles_per_seq, 0, 0, 0)),
            pl.BlockSpec((slab_a, w_next_a.shape[1]), lambda s: (cur(s), 0)),
            pl.BlockSpec((slab_b, w_next_b.shape[1]), lambda s: (cur(s), 0)),
        ],
        out_shape=[
            jax.ShapeDtypeStruct((n, D_MODEL), F32),
            jax.ShapeDtypeStruct((n // seq, HEADS, DK, DV), F32),
            jax.ShapeDtypeStruct(w_next_a.shape, BF16),
            jax.ShapeDtypeStruct(w_next_b.shape, BF16),
        ],
        scratch_shapes=2 * [
            pltpu.VMEM((TILE, QKW), BF16),
            pltpu.VMEM((QKW, TILE), F32),
            pltpu.VMEM((TILE, VW), BF16),
            pltpu.VMEM((TILE, VW), F32),
        ] + [pltpu.VMEM((TILE, VW), BF16)],
        compiler_params=pltpu.CompilerParams(
            dimension_semantics=("arbitrary",),
            vmem_limit_bytes=VMEM_LIMIT),
        name="ret_prompt",
    )(gl, x, x, cos_r, sin_r, cos_t, sin_t, w_in, decay, cross, kd, gn_gain, w_out, ln_g, ln_b,
      w_next_a, w_next_b)


def _mlp_project_gate(j, xb, win_ref, p_scr):
    u = _dot(xb, win_ref[:, j * DV:(j + 1) * DV])
    g = _dot(xb, win_ref[:, 2 * MLP_W + j * DV:2 * MLP_W + (j + 1) * DV])
    p_scr[:, j * DV:(j + 1) * DV] = _gelu(u) * _silu(g)


def _mlp_project_v(j, xb, win_ref, gv_scr):
    gv_scr[:, j * DV:(j + 1) * DV] = _gelu(_dot(xb, win_ref[:, MLP_W + j * DV:MLP_W + (j + 1) * DV]))


def _causal_mix_weights(ws_ref):
    row = lax.broadcasted_iota(jnp.int32, (CHUNK, CHUNK), 0)
    col = lax.broadcasted_iota(jnp.int32, (CHUNK, CHUNK), 1)
    return [jnp.where(row >= col, ws_ref[g], 0.0).astype(BF16) for g in range(GROUPS)]


def _mlp_mix(c, ws, p_scr, gv_scr, yin_scr, lngm_ref, lnbm_ref, bsp_ref):
    rows = slice(c * CHUNK, (c + 1) * CHUNK)
    vn = (_norm_rows(gv_scr[rows, :]) * lngm_ref[...] + lnbm_ref[...]).astype(BF16)
    for g in range(GROUPS):
        gs = slice(g * GW, (g + 1) * GW)
        bias = jnp.tile(bsp_ref[:, g * 128:(g + 1) * 128], (1, GW // 128))
        mixed = _dot(ws[g], vn[:, gs]) + bias
        yin_scr[rows, gs] = (p_scr[rows, gs] * mixed).astype(BF16)


def _mlp_out(x_ref, yin_scr, wout_ref, lng_ref, lnb_ref, y_ref):
    z = ALPHA * x_ref[...] + _dot(yin_scr[...], wout_ref[...])
    y_ref[...] = _norm_rows(z) * lng_ref[...] + lnb_ref[...]


def _mlp_prompt_kernel(x_ref, xp_ref, win_ref, lngm_ref, lnbm_ref, ws_ref, bsp_ref, wout_ref,
                       lng_ref, lnb_ref,
                       y_ref,
                       pa, gva, pb, gvb, yin_scr, *, tiles):
    s = pl.program_id(0)

    def step(cur, old):
        stages = {}
        if cur is not None:
            xb = x_ref[...].astype(BF16)
            for j in range(MLP_W // DV):
                stages[f"gate{j}"] = functools.partial(_mlp_project_gate, j, xb, win_ref, cur[0])
                stages[f"v{j}"] = functools.partial(_mlp_project_v, j, xb, win_ref, cur[1])
        if old is not None:
            ws = _causal_mix_weights(ws_ref)
            for c in range(MLP_TILE // CHUNK):
                stages[f"mix{c}"] = functools.partial(_mlp_mix, c, ws, *old, yin_scr, lngm_ref, lnbm_ref,
                                                      bsp_ref)
            stages["out"] = functools.partial(_mlp_out, xp_ref, yin_scr, wout_ref, lng_ref, lnb_ref, y_ref)
        assert set(stages) <= set(MLP_STAGE_ORDER)
        for name in MLP_STAGE_ORDER:
            if name in stages:
                stages[name]()

    sets = ((pa, gva), (pb, gvb))

    @pl.when(s == 0)
    def _():
        step(sets[0], None)

    for parity in range(2):
        @pl.when(jnp.logical_and(jnp.logical_and(s > 0, s < tiles), s % 2 == parity))
        def _():
            step(sets[parity], sets[1 - parity])

    @pl.when(s == tiles)
    def _():
        step(None, sets[(tiles - 1) % 2])


def _mlp_prompt(x, w_in, lng_m, lnb_m, w_s, bsp, w_out, ln_g, ln_b):
    n = x.shape[0]
    tiles = n // MLP_TILE
    cur = lambda s: jnp.minimum(s, tiles - 1)
    prev = lambda s: jnp.maximum(s - 1, 0)
    return pl.pallas_call(
        functools.partial(_mlp_prompt_kernel, tiles=tiles),
        grid=(tiles + 1,),
        in_specs=[
            pl.BlockSpec((MLP_TILE, D_MODEL), lambda s: (cur(s), 0)),
            pl.BlockSpec((MLP_TILE, D_MODEL), lambda s: (prev(s), 0)),
            _const_spec(D_MODEL, 3 * MLP_W),
            _const_spec(1, MLP_W),
            _const_spec(1, MLP_W),
            _const_spec(GROUPS, CHUNK, CHUNK),
            _const_spec(CHUNK, GROUPS * 128),
            _const_spec(MLP_W, D_MODEL),
            _const_spec(1, D_MODEL),
            _const_spec(1, D_MODEL),
        ],
        out_specs=pl.BlockSpec((MLP_TILE, D_MODEL), lambda s: (prev(s), 0)),
        out_shape=jax.ShapeDtypeStruct((n, D_MODEL), F32),
        scratch_shapes=2 * [
            pltpu.VMEM((MLP_TILE, MLP_W), F32),
            pltpu.VMEM((MLP_TILE, MLP_W), F32),
        ] + [pltpu.VMEM((MLP_TILE, MLP_W), BF16)],
        compiler_params=pltpu.CompilerParams(
            dimension_semantics=("arbitrary",),
            vmem_limit_bytes=VMEM_LIMIT),
        name="mlp_prompt",
    )(x, x, w_in, lng_m, lnb_m, w_s, bsp, w_out, ln_g, ln_b)


SAMPLE_COLS = 1024


def _ret_sample_in_kernel(x_ref, cos_ref, sin_ref, win_ref, wo_ref,
                          q_ref, k_ref, v_ref, sg_ref, oin_ref, win16_ref, wo16_ref, xb_scr):
    i = pl.program_id(0)

    @pl.when(i == 0)
    def _():
        xb_scr[...] = x_ref[:, 0, :].astype(BF16)

    w16 = win_ref[...].astype(BF16)
    win16_ref[...] = w16
    wo16_ref[...] = wo_ref[...].astype(BF16)
    h = _dot(xb_scr[...], w16)
    cos = cos_ref[...]
    sin = sin_ref[...]
    q_steps = QKW // SAMPLE_COLS
    v_steps = VW // SAMPLE_COLS
    qk_heads = SAMPLE_COLS // DK
    v_heads = SAMPLE_COLS // DV

    @pl.when(i < q_steps)
    def _():
        for j in range(qk_heads):
            q_ref[i * qk_heads + j] = _rotate(h[:, j * DK:(j + 1) * DK], cos, sin)

    @pl.when(jnp.logical_and(i >= q_steps, i < 2 * q_steps))
    def _():
        for j in range(qk_heads):
            k_ref[(i - q_steps) * qk_heads + j] = _rotate(h[:, j * DK:(j + 1) * DK], cos, sin) * (DK ** -0.5)

    @pl.when(jnp.logical_and(i >= 2 * q_steps, i < 2 * q_steps + v_steps))
    def _():
        for j in range(v_heads):
            v_ref[(i - 2 * q_steps) * v_heads + j] = h[:, j * DV:(j + 1) * DV]

    @pl.when(i >= 2 * q_steps + v_steps)
    def _():
        for j in range(v_heads):
            sg_ref[(i - 2 * q_steps - v_steps) * v_heads + j] = _silu(h[:, j * DV:(j + 1) * DV])

    @pl.when(i == pl.num_programs(0) - 1)
    def _():
        for hd in range(HEADS):
            qk = jnp.sum(q_ref[hd].astype(BF16).astype(F32) * k_ref[hd].astype(BF16).astype(F32),
                         axis=-1, keepdims=True)
            oin_ref[hd] = qk.astype(BF16).astype(F32) * v_ref[hd].astype(BF16).astype(F32)


def _ret_sample_in(x, cos, sin, w_in, w_out):
    n = x.shape[0]
    steps = w_in.shape[1] // SAMPLE_COLS
    wo_rows = 512
    wo_steps = w_out.shape[0] // wo_rows
    assert QKW % SAMPLE_COLS == 0 and SAMPLE_COLS % DV == 0 and wo_steps <= steps
    whole = lambda *shape: pl.BlockSpec(shape, lambda i: (0,) * len(shape))
    wo_spec = pl.BlockSpec((wo_rows, w_out.shape[1]), lambda i: (jnp.minimum(i, wo_steps - 1), 0))
    return pl.pallas_call(
        _ret_sample_in_kernel,
        grid=(steps,),
        in_specs=[
            whole(n, 1, D_MODEL),
            whole(1, ROPE_HALF),
            whole(1, ROPE_HALF),
            pl.BlockSpec((D_MODEL, SAMPLE_COLS), lambda i: (0, i)),
            wo_spec,
        ],
        out_specs=[
            whole(HEADS, n, DK),
            whole(HEADS, n, DK),
            whole(HEADS, n, DV),
            whole(HEADS, n, DV),
            whole(HEADS, n, DV),
            pl.BlockSpec((D_MODEL, SAMPLE_COLS), lambda i: (0, i)),
            wo_spec,
        ],
        out_shape=[
            jax.ShapeDtypeStruct((HEADS, n, DK), F32),
            jax.ShapeDtypeStruct((HEADS, n, DK), F32),
            jax.ShapeDtypeStruct((HEADS, n, DV), F32),
            jax.ShapeDtypeStruct((HEADS, n, DV), F32),
            jax.ShapeDtypeStruct((HEADS, n, DV), F32),
            jax.ShapeDtypeStruct(w_in.shape, BF16),
            jax.ShapeDtypeStruct(w_out.shape, BF16),
        ],
        scratch_shapes=[pltpu.VMEM((n, D_MODEL), BF16)],
        compiler_params=pltpu.CompilerParams(
            dimension_semantics=("arbitrary",),
            vmem_limit_bytes=VMEM_LIMIT),
        name="ret_sample_in",
    )(x, cos, sin, w_in, w_out)


SC_LANES = 16
SC_ROWS = 32
SC_HALF = 256


def _state_sc_kernel(gam_hbm, s0_hbm, k_hbm, q_hbm, v_hbm, s1_hbm, o_hbm,
                     gam_v, v_v, o_v, k_v, q_v, s0_v, s1_v, *, pairs_per_worker, num_cores, n):
    wid = lax.axis_index("s") * num_cores + lax.axis_index("c")
    first = wid * pairs_per_worker
    head = first // n
    pltpu.sync_copy(gam_hbm, gam_v)
    pltpu.sync_copy(v_hbm.at[pl.ds(first, pairs_per_worker)], v_v)
    pltpu.sync_copy(k_hbm.at[pl.ds(first, pairs_per_worker)], k_v)
    pltpu.sync_copy(q_hbm.at[pl.ds(first, pairs_per_worker)], q_v)
    gam = gam_v[head, pl.ds(0, SC_LANES)]

    @pl.loop(0, pairs_per_worker)
    def _(pi):
        seq = first + pi - head * n
        pi_vec = jnp.full((SC_LANES,), pi, jnp.int32)
        for c in range(DV // SC_LANES):
            o_v[pi, pl.ds(c * SC_LANES, SC_LANES)] = jnp.zeros((SC_LANES,), F32)

        @pl.loop(0, DK // SC_ROWS)
        def _(blk):
            row0 = (seq * HEADS + head) * DK + blk * SC_ROWS
            pltpu.sync_copy(s0_hbm.at[pl.ds(row0, SC_ROWS)], s0_v)
            for half in range(DV // SC_HALF):
                cols = [half * SC_HALF + c * SC_LANES for c in range(SC_HALF // SC_LANES)]
                vregs = [v_v[pi, pl.ds(c, SC_LANES)] for c in cols]

                @pl.loop(0, SC_ROWS // 8)
                def _(grp):
                    acc = [None] * len(cols)
                    for j in range(8):
                        d = grp * 8 + j
                        d_vec = jnp.full((SC_LANES,), blk * SC_ROWS + d, jnp.int32)
                        kv = plsc.load_gather(k_v, [pi_vec, d_vec])
                        qv = plsc.load_gather(q_v, [pi_vec, d_vec])
                        for i, (c, vreg) in enumerate(zip(cols, vregs)):
                            s0 = s0_v[d, pl.ds(c, SC_LANES)]
                            s1_v[d, pl.ds(c, SC_LANES)] = gam * s0 + kv * vreg
                            acc[i] = qv * s0 if j == 0 else acc[i] + qv * s0
                    for c, a in zip(cols, acc):
                        plsc.addupdate(o_v.at[pi, pl.ds(c, SC_LANES)], a)
            pltpu.sync_copy(s1_v, s1_hbm.at[pl.ds(row0, SC_ROWS)])

    pltpu.sync_copy(o_v, o_hbm.at[pl.ds(first, pairs_per_worker)])


def _state_sc(gamma, s0, q, k, v):
    n = s0.shape[0]
    info = plsc.get_sparse_core_info()
    workers = info.num_cores * info.num_subcores
    pairs = n * HEADS
    pairs_per_worker = pairs // workers
    assert info.num_lanes == SC_LANES and pairs % workers == 0 and n % pairs_per_worker == 0
    gam_rows = jnp.broadcast_to(jnp.pad(gamma, (0, 8 - HEADS))[:, None], (8, 128))
    mesh = plsc.VectorSubcoreMesh(core_axis_name="c", subcore_axis_name="s")
    s1, o = pl.kernel(
        functools.partial(_state_sc_kernel, pairs_per_worker=pairs_per_worker, num_cores=info.num_cores, n=n),
        out_type=[
            jax.ShapeDtypeStruct((pairs * DK, DV), F32),
            jax.ShapeDtypeStruct((pairs, DV), F32),
        ],
        mesh=mesh,
        scratch_types=[
            pltpu.VMEM((8, 128), F32),
            pltpu.VMEM((pairs_per_worker, DV), F32),
            pltpu.VMEM((pairs_per_worker, DV), F32),
            pltpu.VMEM((pairs_per_worker, DK), F32),
            pltpu.VMEM((pairs_per_worker, DK), F32),
            pltpu.VMEM((SC_ROWS, DV), F32),
            pltpu.VMEM((SC_ROWS, DV), F32),
        ],
        compiler_params=pltpu.CompilerParams(use_tc_tiling_on_sc=True, needs_layout_passes=False),
        name="state_sc",
    )(gam_rows, s0.reshape(pairs * DK, DV), k.reshape(pairs, DK), q.reshape(pairs, DK), v.reshape(pairs, DV))
    return s1.reshape(s0.shape), o.reshape(HEADS, n, DV)


def _sample_finish_kernel(gam_ref, x_ref, oin_ref, oc_ref, sg_ref, gn_ref, wout0_ref, lng0_ref, lnb0_ref,
                          win_ref, lngm_ref, lnbm_ref, wsd_ref, bsd_ref, wout1_ref, lng1_ref, lnb1_ref,
                          y_ref, vout_ref, yin_scr):
    x = x_ref[:, 0, :]
    for h in range(HEADS):
        vs = slice(h * DV, (h + 1) * DV)
        o = oin_ref[h] + oc_ref[h] * gam_ref[h]
        yin_scr[:, vs] = (sg_ref[h] * (_norm_rows(o) * gn_ref[:, vs])).astype(BF16)
    z = ALPHA * x + _dot(yin_scr[...], wout0_ref[...])
    x1 = _norm_rows(z) * lng0_ref[...] + lnb0_ref[...]

    xb = x1.astype(BF16)
    u = _gelu(_dot(xb, win_ref[:, :MLP_W]))
    v = _gelu(_dot(xb, win_ref[:, MLP_W:2 * MLP_W]))
    g = _dot(xb, win_ref[:, 2 * MLP_W:])
    vn = _norm_rows(v) * lngm_ref[...] + lnbm_ref[...]
    vout_ref[0, :, 0, :] = vn
    mixed = wsd_ref[...].astype(BF16).astype(F32) * vn.astype(BF16).astype(F32) + bsd_ref[...]
    yin = (u * mixed * _silu(g)).astype(BF16)
    z1 = ALPHA * x1 + _dot(yin, wout1_ref[...])
    y_ref[:, 0, :] = _norm_rows(z1) * lng1_ref[...] + lnb1_ref[...]


def _sample_finish(gamma, x, oin, oc, sg, gn_gain, w_out0, lng0, lnb0,
                   w_in1, lng_m, lnb_m, wsd, bsd, w_out1, lng1, lnb1):
    n = x.shape[0]
    vmem = pl.BlockSpec(memory_space=pltpu.VMEM)
    return pl.pallas_call(
        _sample_finish_kernel,
        in_specs=[pl.BlockSpec(memory_space=pltpu.SMEM)] + [vmem] * 16,
        out_shape=[
            jax.ShapeDtypeStruct((n, 1, D_MODEL), F32),
            jax.ShapeDtypeStruct((1, n, 1, MLP_W), F32),
        ],
        scratch_shapes=[pltpu.VMEM((n, VW), BF16)],
        compiler_params=pltpu.CompilerParams(vmem_limit_bytes=VMEM_LIMIT),
        name="sample_finish",
    )(gamma, x, oin, oc, sg, gn_gain, w_out0, lng0, lnb0,
      w_in1, lng_m, lnb_m, wsd, bsd, w_out1, lng1, lnb1)


def _rope_tables(pos):
    inv = ROPE_BASE ** (-jnp.arange(ROPE_HALF, dtype=F32) / ROPE_HALF)
    ang = pos.astype(F32)[:, None] * inv[None, :]
    return jnp.cos(ang), jnp.sin(ang)


def _decay_tables():
    lg = jnp.log1p(-jnp.exp2(-5.0 - jnp.arange(HEADS, dtype=F32)))
    idx = jnp.arange(RET_BLOCK, dtype=F32)
    diff = idx[:, None] - idx[None, :]
    decay = jnp.where(diff >= 0, jnp.exp(jnp.maximum(diff, 0.0)[None] * lg[:, None, None]), 0.0)
    cross = jnp.exp((idx[:, None] + 1.0) * lg[None, :])
    kd = jnp.exp((RET_BLOCK - 1.0 - idx)[:, None] * lg[None, :])
    cross = jnp.broadcast_to(cross.T[:, :, None], (HEADS, RET_BLOCK, 128))
    kd = kd.T[:, None, :]
    return decay, cross, kd, jnp.exp(RET_BLOCK * lg), jnp.exp(lg)


def kernel(x_prompt, x_sample, state_ret, ln_gain, ln_bias, w_in_ret, gn_gain_ret, w_out_ret,
           w_in_mlp, ln_gain_mlp, ln_bias_mlp, w_spatial, b_spatial, w_out_mlp):
    batch, seq, _ = x_prompt.shape
    n_dec = x_sample.shape[0]
    assert x_sample.shape[1] == 1 and seq % TILE == 0 and seq % MLP_TILE == 0

    lng0, lnb0 = ln_gain[0][None], ln_bias[0][None]
    lng1, lnb1 = ln_gain[1][None], ln_bias[1][None]
    gn_gain = gn_gain_ret[0][None]
    lng_m, lnb_m = ln_gain_mlp[0][None], ln_bias_mlp[0][None]

    rope = (*_rope_tables(jnp.arange(TILE, dtype=jnp.int32)),
            *_rope_tables(TILE * jnp.arange(seq // TILE, dtype=jnp.int32)))
    cos_s, sin_s = _rope_tables(PAST_LEN + jnp.arange(1, dtype=jnp.int32))
    decay, cross, kd, gamma_chunk, gamma = _decay_tables()

    q, k, v, sg, oin, w_in0, w_out0 = _ret_sample_in(x_sample, cos_s, sin_s, w_in_ret[0], w_out_ret[0])

    x1, ret_state_prompt, w_in1, w_out1 = _ret_prompt(
        x_prompt.reshape(batch * seq, D_MODEL), rope, w_in0, decay, cross, kd, gamma_chunk,
        gn_gain, w_out0, lng0, lnb0, w_in_mlp[0], w_out_mlp[0], seq=seq)
    bsp = jnp.repeat(jnp.transpose(b_spatial[0]), 128, axis=1)
    y_prompt = _mlp_prompt(x1, w_in1, lng_m, lnb_m, w_spatial[0], bsp, w_out1, lng1, lnb1)
    ret_state_sample, oc = _state_sc(gamma, state_ret[0], q, k, v)

    wsd = jnp.repeat(w_spatial[0, :, 0, 0], GW)[None]
    bsd = jnp.repeat(b_spatial[0, :, 0], GW)[None]
    y_sample, mlp_v = _sample_finish(gamma, x_sample, oin, oc, sg, gn_gain, w_out0, lng0, lnb0,
                                     w_in1, lng_m, lnb_m, wsd, bsd, w_out1, lng1, lnb1)

    return (y_prompt.reshape(batch, seq, D_MODEL),
            y_sample,
            ret_state_prompt[None],
            ret_state_sample[None],
            mlp_v)
```

```python
import functools

import jax
import jax.numpy as jnp
import numpy as np
from jax import lax
from jax.experimental import pallas as pl
from jax.experimental.pallas import tpu as pltpu
from jax.experimental.pallas import tpu_sc as plsc

F32 = jnp.float32
BF16 = jnp.bfloat16

D_MODEL = 1024
DEPTH = 2
PAST_LEN = 16384
HEADS = 4
DK = D_MODEL // HEADS
DV = 2 * DK
QKW = HEADS * DK
VW = HEADS * DV
CHUNK = 128
ROPE_BASE = 10000.0
ROPE_HALF = DK // 2
MLP_W = 2 * D_MODEL
GROUPS = 8
GW = MLP_W // GROUPS
ALPHA = (2 * DEPTH) ** 0.25
LN_EPS = 1e-5
SQRT_HALF = float(np.sqrt(0.5))

TILE = 256
RET_BLOCK = TILE
MLP_TILE = 256
RET_STAGE_ORDER = ("qk0", "qk1", "qk2", "qk3", "ret0", "ret1", "ret2", "ret3", "out",
                   "gate0", "gate1", "gate2", "gate3", "v0", "v1", "v2", "v3")
MLP_STAGE_ORDER = ("gate0", "v0", "mix0", "gate1", "v1", "mix1", "gate2", "v2", "out", "gate3", "v3")
VMEM_LIMIT = 60 * 1024 * 1024


def _dot(a, b):
    return jnp.dot(a, b, preferred_element_type=F32)


def _gelu(x):
    return 0.5 * x * (1.0 + lax.erf(x * SQRT_HALF))


def _silu(x):
    return x * jax.nn.sigmoid(x)


def _norm_rows(x):
    mu = jnp.mean(x, axis=-1, keepdims=True)
    d = x - mu
    var = jnp.mean(d * d, axis=-1, keepdims=True)
    return d * lax.rsqrt(var + LN_EPS)


def _rotate(h, cos, sin):
    x1 = h[:, :ROPE_HALF]
    x2 = h[:, ROPE_HALF:]
    return jnp.concatenate([x1 * cos - x2 * sin, x2 * cos + x1 * sin], axis=-1)


def _const_spec(*shape):
    return pl.BlockSpec(shape, lambda s: (0,) * len(shape), pipeline_mode=pl.Buffered(1))


def _ret_project_qk(h, xb, cos, sin, win_ref, q_scr, k_scr):
    qs = slice(h * DK, (h + 1) * DK)
    q_scr[:, qs] = _rotate(_dot(xb, win_ref[:, qs]), cos, sin).astype(BF16)
    ks = slice(QKW + h * DK, QKW + (h + 1) * DK)
    k_scr[qs, :] = (_rotate(_dot(xb, win_ref[:, ks]), cos, sin) * (DK ** -0.5)).T


def _ret_project_gate(h, xb, win_ref, sg_scr):
    vs = slice(h * DV, (h + 1) * DV)
    sg_scr[:, vs] = _silu(_dot(xb, win_ref[:, 2 * QKW + VW + h * DV:2 * QKW + VW + (h + 1) * DV]))


def _ret_project_v(h, xb, win_ref, v_scr):
    vs = slice(h * DV, (h + 1) * DV)
    v_scr[:, vs] = _dot(xb, win_ref[:, 2 * QKW + h * DV:2 * QKW + (h + 1) * DV]).astype(BF16)


def _ret_head(h, q_scr, k_scr, v_scr, sg_scr, yin_scr, st_ref, gl_ref, decay_ref, cross_ref, kd_ref, gn_ref):
    qs = slice(h * DK, (h + 1) * DK)
    vs = slice(h * DV, (h + 1) * DV)
    for c in range(TILE // RET_BLOCK):
        rows = slice(c * RET_BLOCK, (c + 1) * RET_BLOCK)
        qc = q_scr[rows, qs]
        kt = k_scr[qs, rows]
        vc = v_scr[rows, vs]
        s = _dot(qc, kt.astype(BF16)) * decay_ref[h]
        state = st_ref[h]
        o = _dot(s.astype(BF16), vc) + _dot(qc, state.astype(BF16)) * jnp.tile(cross_ref[h], (1, DV // 128))
        kdec = (kt * kd_ref[h]).astype(BF16)
        st_ref[h] = gl_ref[h] * state + _dot(kdec, vc)
        on = _norm_rows(o) * gn_ref[:, vs]
        yin_scr[rows, vs] = (sg_scr[rows, vs] * on).astype(BF16)


def _ret_out(x_ref, yin_scr, wout_ref, lng_ref, lnb_ref, y_ref):
    z = ALPHA * x_ref[...] + _dot(yin_scr[...], wout_ref[...])
    y_ref[...] = _norm_rows(z) * lng_ref[...] + lnb_ref[...]


def _ret_prompt_kernel(gl_ref, x_ref, xp_ref, cosr_ref, sinr_ref, cost_ref, sint_ref, win_ref,
                       decay_ref, cross_ref, kd_ref, gn_ref, wout_ref, lng_ref, lnb_ref, wa_ref, wb_ref,
                       y_ref, st_ref, wa16_ref, wb16_ref,
                       qa, ka, va, sga, qb, kb, vb, sgb, yin_scr, *, tiles, tiles_per_seq):
    s = pl.program_id(0)
    prev = jnp.maximum(s - 1, 0)
    tile_in_seq = jnp.minimum(s, tiles - 1) % tiles_per_seq

    wa16_ref[...] = wa_ref[...].astype(BF16)
    wb16_ref[...] = wb_ref[...].astype(BF16)

    @pl.when(prev % tiles_per_seq == 0)
    def _():
        st_ref[...] = jnp.zeros_like(st_ref)

    def step(cur, old):
        stages = {}
        if cur is not None:
            ct = cost_ref[pl.ds(tile_in_seq, 1), :]
            st = sint_ref[pl.ds(tile_in_seq, 1), :]
            cos = ct * cosr_ref[...] - st * sinr_ref[...]
            sin = st * cosr_ref[...] + ct * sinr_ref[...]
            xb = x_ref[...].astype(BF16)
            for h in range(HEADS):
                stages[f"qk{h}"] = functools.partial(_ret_project_qk, h, xb, cos, sin, win_ref, *cur[:2])
                stages[f"gate{h}"] = functools.partial(_ret_project_gate, h, xb, win_ref, cur[3])
                stages[f"v{h}"] = functools.partial(_ret_project_v, h, xb, win_ref, cur[2])
        if old is not None:
            for h in range(HEADS):
                stages[f"ret{h}"] = functools.partial(_ret_head, h, *old, yin_scr, st_ref, gl_ref, decay_ref,
                                                      cross_ref, kd_ref, gn_ref)
            stages["out"] = functools.partial(_ret_out, xp_ref, yin_scr, wout_ref, lng_ref, lnb_ref, y_ref)
        assert set(stages) <= set(RET_STAGE_ORDER)
        for name in RET_STAGE_ORDER:
            if name in stages:
                stages[name]()

    sets = ((qa, ka, va, sga), (qb, kb, vb, sgb))

    @pl.when(s == 0)
    def _():
        step(sets[0], None)

    for parity in range(2):
        @pl.when(jnp.logical_and(jnp.logical_and(s > 0, s < tiles), s % 2 == parity))
        def _():
            step(sets[parity], sets[1 - parity])

    @pl.when(s == tiles)
    def _():
        step(None, sets[(tiles - 1) % 2])


def _ret_prompt(x, rope, w_in, decay, cross, kd, gl, gn_gain, w_out, ln_g, ln_b, w_next_a, w_next_b, *, seq):
    cos_r, sin_r, cos_t, sin_t = rope
    n = x.shape[0]
    tiles = n // TILE
    tiles_per_seq = seq // TILE
    cur = lambda s: jnp.minimum(s, tiles - 1)
    prev = lambda s: jnp.maximum(s - 1, 0)
    slab_a = w_next_a.shape[0] // tiles
    slab_b = w_next_b.shape[0] // tiles
    assert slab_a * tiles == w_next_a.shape[0] and slab_b * tiles == w_next_b.shape[0]
    return pl.pallas_call(
        functools.partial(_ret_prompt_kernel, tiles=tiles, tiles_per_seq=tiles_per_seq),
        grid=(tiles + 1,),
        in_specs=[
            pl.BlockSpec(memory_space=pltpu.SMEM),
            pl.BlockSpec((TILE, D_MODEL), lambda s: (cur(s), 0)),
            pl.BlockSpec((TILE, D_MODEL), lambda s: (prev(s), 0)),
            _const_spec(TILE, ROPE_HALF),
            _const_spec(TILE, ROPE_HALF),
            _const_spec(tiles_per_seq, ROPE_HALF),
            _const_spec(tiles_per_seq, ROPE_HALF),
            _const_spec(D_MODEL, 2 * QKW + 2 * VW),
            _const_spec(HEADS, RET_BLOCK, RET_BLOCK),
            _const_spec(HEADS, RET_BLOCK, 128),
            _const_spec(HEADS, 1, RET_BLOCK),
            _const_spec(1, VW),
            _const_spec(VW, D_MODEL),
            _const_spec(1, D_MODEL),
            _const_spec(1, D_MODEL),
            pl.BlockSpec((slab_a, w_next_a.shape[1]), lambda s: (cur(s), 0)),
            pl.BlockSpec((slab_b, w_next_b.shape[1]), lambda s: (cur(s), 0)),
        ],
        out_specs=[
            pl.BlockSpec((TILE, D_MODEL), lambda s: (prev(s), 0)),
            pl.BlockSpec((None, HEADS, DK, DV), lambda s: (prev(s) // tiles_per_seq, 0, 0, 0)),
            pl.BlockSpec((slab_a, w_next_a.shape[1]), lambda s: (cur(s), 0)),
            pl.BlockSpec((slab_b, w_next_b.shape[1]), lambda s: (cur(s), 0)),
        ],
        out_shape=[
            jax.ShapeDtypeStruct((n, D_MODEL), F32),
            jax.ShapeDtypeStruct((n // seq, HEADS, DK, DV), F32),
            jax.ShapeDtypeStruct(w_next_a.shape, BF16),
            jax.ShapeDtypeStruct(w_next_b.shape, BF16),
        ],
        scratch_shapes=2 * [
            pltpu.VMEM((TILE, QKW), BF16),
            pltpu.VMEM((QKW, TILE), F32),
            pltpu.VMEM((TILE, VW), BF16),
            pltpu.VMEM((TILE, VW), F32),
        ] + [pltpu.VMEM((TILE, VW), BF16)],
        compiler_params=pltpu.CompilerParams(
            dimension_semantics=("arbitrary",),
            vmem_limit_bytes=VMEM_LIMIT),
        name="ret_prompt",
    )(gl, x, x, cos_r, sin_r, cos_t, sin_t, w_in, decay, cross, kd, gn_gain, w_out, ln_g, ln_b,
      w_next_a, w_next_b)


def _mlp_project_gate(j, xb, win_ref, p_scr):
    u = _dot(xb, win_ref[:, j * DV:(j + 1) * DV])
    g = _dot(xb, win_ref[:, 2 * MLP_W + j * DV:2 * MLP_W + (j + 1) * DV])
    p_scr[:, j * DV:(j + 1) * DV] = _gelu(u) * _silu(g)


def _mlp_project_v(j, xb, win_ref, gv_scr):
    gv_scr[:, j * DV:(j + 1) * DV] = _gelu(_dot(xb, win_ref[:, MLP_W + j * DV:MLP_W + (j + 1) * DV]))


def _causal_mix_weights(ws_ref):
    row = lax.broadcasted_iota(jnp.int32, (CHUNK, CHUNK), 0)
    col = lax.broadcasted_iota(jnp.int32, (CHUNK, CHUNK), 1)
    return [jnp.where(row >= col, ws_ref[g], 0.0).astype(BF16) for g in range(GROUPS)]


def _mlp_mix(c, ws, p_scr, gv_scr, yin_scr, lngm_ref, lnbm_ref, bsp_ref):
    rows = slice(c * CHUNK, (c + 1) * CHUNK)
    vn = (_norm_rows(gv_scr[rows, :]) * lngm_ref[...] + lnbm_ref[...]).astype(BF16)
    for g in range(GROUPS):
        gs = slice(g * GW, (g + 1) * GW)
        bias = jnp.tile(bsp_ref[:, g * 128:(g + 1) * 128], (1, GW // 128))
        mixed = _dot(ws[g], vn[:, gs]) + bias
        yin_scr[rows, gs] = (p_scr[rows, gs] * mixed).astype(BF16)


def _mlp_out(x_ref, yin_scr, wout_ref, lng_ref, lnb_ref, y_ref):
    z = ALPHA * x_ref[...] + _dot(yin_scr[...], wout_ref[...])
    y_ref[...] = _norm_rows(z) * lng_ref[...] + lnb_ref[...]


def _mlp_prompt_kernel(x_ref, xp_ref, win_ref, lngm_ref, lnbm_ref, ws_ref, bsp_ref, wout_ref,
                       lng_ref, lnb_ref,
                       y_ref,
                       pa, gva, pb, gvb, yin_scr, *, tiles):
    s = pl.program_id(0)

    def step(cur, old):
        stages = {}
        if cur is not None:
            xb = x_ref[...].astype(BF16)
            for j in range(MLP_W // DV):
                stages[f"gate{j}"] = functools.partial(_mlp_project_gate, j, xb, win_ref, cur[0])
                stages[f"v{j}"] = functools.partial(_mlp_project_v, j, xb, win_ref, cur[1])
        if old is not None:
            ws = _causal_mix_weights(ws_ref)
            for c in range(MLP_TILE // CHUNK):
                stages[f"mix{c}"] = functools.partial(_mlp_mix, c, ws, *old, yin_scr, lngm_ref, lnbm_ref,
                                                      bsp_ref)
            stages["out"] = functools.partial(_mlp_out, xp_ref, yin_scr, wout_ref, lng_ref, lnb_ref, y_ref)
        assert set(stages) <= set(MLP_STAGE_ORDER)
        for name in MLP_STAGE_ORDER:
            if name in stages:
                stages[name]()

    sets = ((pa, gva), (pb, gvb))

    @pl.when(s == 0)
    def _():
        step(sets[0], None)

    for parity in range(2):
        @pl.when(jnp.logical_and(jnp.logical_and(s > 0, s < tiles), s % 2 == parity))
        def _():
            step(sets[parity], sets[1 - parity])

    @pl.when(s == tiles)
    def _():
        step(None, sets[(tiles - 1) % 2])


def _mlp_prompt(x, w_in, lng_m, lnb_m, w_s, bsp, w_out, ln_g, ln_b):
    n = x.shape[0]
    tiles = n // MLP_TILE
    cur = lambda s: jnp.minimum(s, tiles - 1)
    prev = lambda s: jnp.maximum(s - 1, 0)
    return pl.pallas_call(
        functools.partial(_mlp_prompt_kernel, tiles=tiles),
        grid=(tiles + 1,),
        in_specs=[
            pl.BlockSpec((MLP_TILE, D_MODEL), lambda s: (cur(s), 0)),
            pl.BlockSpec((MLP_TILE, D_MODEL), lambda s: (prev(s), 0)),
            _const_spec(D_MODEL, 3 * MLP_W),
            _const_spec(1, MLP_W),
            _const_spec(1, MLP_W),
            _const_spec(GROUPS, CHUNK, CHUNK),
            _const_spec(CHUNK, GROUPS * 128),
            _const_spec(MLP_W, D_MODEL),
            _const_spec(1, D_MODEL),
            _const_spec(1, D_MODEL),
        ],
        out_specs=pl.BlockSpec((MLP_TILE, D_MODEL), lambda s: (prev(s), 0)),
        out_shape=jax.ShapeDtypeStruct((n, D_MODEL), F32),
        scratch_shapes=2 * [
            pltpu.VMEM((MLP_TILE, MLP_W), F32),
            pltpu.VMEM((MLP_TILE, MLP_W), F32),
        ] + [pltpu.VMEM((MLP_TILE, MLP_W), BF16)],
        compiler_params=pltpu.CompilerParams(
            dimension_semantics=("arbitrary",),
            vmem_limit_bytes=VMEM_LIMIT),
        name="mlp_prompt",
    )(x, x, w_in, lng_m, lnb_m, w_s, bsp, w_out, ln_g, ln_b)


SAMPLE_COLS = 1024


def _ret_sample_in_kernel(x_ref, cos_ref, sin_ref, win_ref, wo_ref,
                          q_ref, k_ref, v_ref, sg_ref, oin_ref, win16_ref, wo16_ref, xb_scr):
    i = pl.program_id(0)

    @pl.when(i == 0)
    def _():
        xb_scr[...] = x_ref[:, 0, :].astype(BF16)

    w16 = win_ref[...].astype(BF16)
    win16_ref[...] = w16
    wo16_ref[...] = wo_ref[...].astype(BF16)
    h = _dot(xb_scr[...], w16)
    cos = cos_ref[...]
    sin = sin_ref[...]
    q_steps = QKW // SAMPLE_COLS
    v_steps = VW // SAMPLE_COLS
    qk_heads = SAMPLE_COLS // DK
    v_heads = SAMPLE_COLS // DV

    @pl.when(i < q_steps)
    def _():
        for j in range(qk_heads):
            q_ref[i * qk_heads + j] = _rotate(h[:, j * DK:(j + 1) * DK], cos, sin)

    @pl.when(jnp.logical_and(i >= q_steps, i < 2 * q_steps))
    def _():
        for j in range(qk_heads):
            k_ref[(i - q_steps) * qk_heads + j] = _rotate(h[:, j * DK:(j + 1) * DK], cos, sin) * (DK ** -0.5)

    @pl.when(jnp.logical_and(i >= 2 * q_steps, i < 2 * q_steps + v_steps))
    def _():
        for j in range(v_heads):
            v_ref[(i - 2 * q_steps) * v_heads + j] = h[:, j * DV:(j + 1) * DV]

    @pl.when(i >= 2 * q_steps + v_steps)
    def _():
        for j in range(v_heads):
            sg_ref[(i - 2 * q_steps - v_steps) * v_heads + j] = _silu(h[:, j * DV:(j + 1) * DV])

    @pl.when(i == pl.num_programs(0) - 1)
    def _():
        for hd in range(HEADS):
            qk = jnp.sum(q_ref[hd].astype(BF16).astype(F32) * k_ref[hd].astype(BF16).astype(F32),
                         axis=-1, keepdims=True)
            oin_ref[hd] = qk.astype(BF16).astype(F32) * v_ref[hd].astype(BF16).astype(F32)


def _ret_sample_in(x, cos, sin, w_in, w_out):
    n = x.shape[0]
    steps = w_in.shape[1] // SAMPLE_COLS
    wo_rows = 512
    wo_steps = w_out.shape[0] // wo_rows
    assert QKW % SAMPLE_COLS == 0 and SAMPLE_COLS % DV == 0 and wo_steps <= steps
    whole = lambda *shape: pl.BlockSpec(shape, lambda i: (0,) * len(shape))
    wo_spec = pl.BlockSpec((wo_rows, w_out.shape[1]), lambda i: (jnp.minimum(i, wo_steps - 1), 0))
    return pl.pallas_call(
        _ret_sample_in_kernel,
        grid=(steps,),
        in_specs=[
            whole(n, 1, D_MODEL),
            whole(1, ROPE_HALF),
            whole(1, ROPE_HALF),
            pl.BlockSpec((D_MODEL, SAMPLE_COLS), lambda i: (0, i)),
            wo_spec,
        ],
        out_specs=[
            whole(HEADS, n, DK),
            whole(HEADS, n, DK),
            whole(HEADS, n, DV),
            whole(HEADS, n, DV),
            whole(HEADS, n, DV),
            pl.BlockSpec((D_MODEL, SAMPLE_COLS), lambda i: (0, i)),
            wo_spec,
        ],
        out_shape=[
            jax.ShapeDtypeStruct((HEADS, n, DK), F32),
            jax.ShapeDtypeStruct((HEADS, n, DK), F32),
            jax.ShapeDtypeStruct((HEADS, n, DV), F32),
            jax.ShapeDtypeStruct((HEADS, n, DV), F32),
            jax.ShapeDtypeStruct((HEADS, n, DV), F32),
            jax.ShapeDtypeStruct(w_in.shape, BF16),
            jax.ShapeDtypeStruct(w_out.shape, BF16),
        ],
        scratch_shapes=[pltpu.VMEM((n, D_MODEL), BF16)],
        compiler_params=pltpu.CompilerParams(
            dimension_semantics=("arbitrary",),
            vmem_limit_bytes=VMEM_LIMIT),
        name="ret_sample_in",
    )(x, cos, sin, w_in, w_out)


SC_LANES = 16
SC_ROWS = 32
SC_HALF = 256


def _state_sc_kernel(gam_hbm, s0_hbm, k_hbm, q_hbm, v_hbm, s1_hbm, o_hbm,
                     gam_v, v_v, o_v, k_v, q_v, s0_v, s1_v, *, pairs_per_worker, num_cores, n):
    wid = lax.axis_index("s") * num_cores + lax.axis_index("c")
    first = wid * pairs_per_worker
    head = first // n
    pltpu.sync_copy(gam_hbm, gam_v)
    pltpu.sync_copy(v_hbm.at[pl.ds(first, pairs_per_worker)], v_v)
    pltpu.sync_copy(k_hbm.at[pl.ds(first, pairs_per_worker)], k_v)
    pltpu.sync_copy(q_hbm.at[pl.ds(first, pairs_per_worker)], q_v)
    gam = gam_v[head, pl.ds(0, SC_LANES)]

    @pl.loop(0, pairs_per_worker)
    def _(pi):
        seq = first + pi - head * n
        pi_vec = jnp.full((SC_LANES,), pi, jnp.int32)
        for c in range(DV // SC_LANES):
            o_v[pi, pl.ds(c * SC_LANES, SC_LANES)] = jnp.zeros((SC_LANES,), F32)

        @pl.loop(0, DK // SC_ROWS)
        def _(blk):
            row0 = (seq * HEADS + head) * DK + blk * SC_ROWS
            pltpu.sync_copy(s0_hbm.at[pl.ds(row0, SC_ROWS)], s0_v)
            for half in range(DV // SC_HALF):
                cols = [half * SC_HALF + c * SC_LANES for c in range(SC_HALF // SC_LANES)]
                vregs = [v_v[pi, pl.ds(c, SC_LANES)] for c in cols]

                @pl.loop(0, SC_ROWS // 8)
                def _(grp):
                    acc = [None] * len(cols)
                    for j in range(8):
                        d = grp * 8 + j
                        d_vec = jnp.full((SC_LANES,), blk * SC_ROWS + d, jnp.int32)
                        kv = plsc.load_gather(k_v, [pi_vec, d_vec])
                        qv = plsc.load_gather(q_v, [pi_vec, d_vec])
                        for i, (c, vreg) in enumerate(zip(cols, vregs)):
                            s0 = s0_v[d, pl.ds(c, SC_LANES)]
                            s1_v[d, pl.ds(c, SC_LANES)] = gam * s0 + kv * vreg
                            acc[i] = qv * s0 if j == 0 else acc[i] + qv * s0
                    for c, a in zip(cols, acc):
                        plsc.addupdate(o_v.at[pi, pl.ds(c, SC_LANES)], a)
            pltpu.sync_copy(s1_v, s1_hbm.at[pl.ds(row0, SC_ROWS)])

    pltpu.sync_copy(o_v, o_hbm.at[pl.ds(first, pairs_per_worker)])


def _state_sc(gamma, s0, q, k, v):
    n = s0.shape[0]
    info = plsc.get_sparse_core_info()
    workers = info.num_cores * info.num_subcores
    pairs = n * HEADS
    pairs_per_worker = pairs // workers
    assert info.num_lanes == SC_LANES and pairs % workers == 0 and n % pairs_per_worker == 0
    gam_rows = jnp.broadcast_to(jnp.pad(gamma, (0, 8 - HEADS))[:, None], (8, 128))
    mesh = plsc.VectorSubcoreMesh(core_axis_name="c", subcore_axis_name="s")
    s1, o = pl.kernel(
        functools.partial(_state_sc_kernel, pairs_per_worker=pairs_per_worker, num_cores=info.num_cores, n=n),
        out_type=[
            jax.ShapeDtypeStruct((pairs * DK, DV), F32),
            jax.ShapeDtypeStruct((pairs, DV), F32),
        ],
        mesh=mesh,
        scratch_types=[
            pltpu.VMEM((8, 128), F32),
            pltpu.VMEM((pairs_per_worker, DV), F32),
            pltpu.VMEM((pairs_per_worker, DV), F32),
            pltpu.VMEM((pairs_per_worker, DK), F32),
            pltpu.VMEM((pairs_per_worker, DK), F32),
            pltpu.VMEM((SC_ROWS, DV), F32),
            pltpu.VMEM((SC_ROWS, DV), F32),
        ],
        compiler_params=pltpu.CompilerParams(use_tc_tiling_on_sc=True, needs_layout_passes=False),
        name="state_sc",
    )(gam_rows, s0.reshape(pairs * DK, DV), k.reshape(pairs, DK), q.reshape(pairs, DK), v.reshape(pairs, DV))
    return s1.reshape(s0.shape), o.reshape(HEADS, n, DV)


def _sample_finish_kernel(gam_ref, x_ref, oin_ref, oc_ref, sg_ref, gn_ref, wout0_ref, lng0_ref, lnb0_ref,
                          win_ref, lngm_ref, lnbm_ref, wsd_ref, bsd_ref, wout1_ref, lng1_ref, lnb1_ref,
                          y_ref, vout_ref, yin_scr):
    x = x_ref[:, 0, :]
    for h in range(HEADS):
        vs = slice(h * DV, (h + 1) * DV)
        o = oin_ref[h] + oc_ref[h] * gam_ref[h]
        yin_scr[:, vs] = (sg_ref[h] * (_norm_rows(o) * gn_ref[:, vs])).astype(BF16)
    z = ALPHA * x + _dot(yin_scr[...], wout0_ref[...])
    x1 = _norm_rows(z) * lng0_ref[...] + lnb0_ref[...]

    xb = x1.astype(BF16)
    u = _gelu(_dot(xb, win_ref[:, :MLP_W]))
    v = _gelu(_dot(xb, win_ref[:, MLP_W:2 * MLP_W]))
    g = _dot(xb, win_ref[:, 2 * MLP_W:])
    vn = _norm_rows(v) * lngm_ref[...] + lnbm_ref[...]
    vout_ref[0, :, 0, :] = vn
    mixed = wsd_ref[...].astype(BF16).astype(F32) * vn.astype(BF16).astype(F32) + bsd_ref[...]
    yin = (u * mixed * _silu(g)).astype(BF16)
    z1 = ALPHA * x1 + _dot(yin, wout1_ref[...])
    y_ref[:, 0, :] = _norm_rows(z1) * lng1_ref[...] + lnb1_ref[...]


def _sample_finish(gamma, x, oin, oc, sg, gn_gain, w_out0, lng0, lnb0,
                   w_in1, lng_m, lnb_m, wsd, bsd, w_out1, lng1, lnb1):
    n = x.shape[0]
    vmem = pl.BlockSpec(memory_space=pltpu.VMEM)
    return pl.pallas_call(
        _sample_finish_kernel,
        in_specs=[pl.BlockSpec(memory_space=pltpu.SMEM)] + [vmem] * 16,
        out_shape=[
            jax.ShapeDtypeStruct((n, 1, D_MODEL), F32),
            jax.ShapeDtypeStruct((1, n, 1, MLP_W), F32),
        ],
        scratch_shapes=[pltpu.VMEM((n, VW), BF16)],
        compiler_params=pltpu.CompilerParams(vmem_limit_bytes=VMEM_LIMIT),
        name="sample_finish",
    )(gamma, x, oin, oc, sg, gn_gain, w_out0, lng0, lnb0,
      w_in1, lng_m, lnb_m, wsd, bsd, w_out1, lng1, lnb1)


def _rope_tables(pos):
    inv = (np.float32(ROPE_BASE) ** (-np.arange(ROPE_HALF, dtype=np.float32) / np.float32(ROPE_HALF)))
    ang = np.asarray(pos, np.float32)[:, None] * inv.astype(np.float32)[None, :]
    return jnp.asarray(np.cos(ang), F32), jnp.asarray(np.sin(ang), F32)


def _decay_tables():
    lg = np.log1p(-np.exp2(-5.0 - np.arange(HEADS, dtype=np.float32))).astype(np.float32)
    idx = np.arange(RET_BLOCK, dtype=np.float32)
    diff = idx[:, None] - idx[None, :]
    decay = np.where(diff >= 0, np.exp(np.maximum(diff, 0.0)[None] * lg[:, None, None]), 0.0)
    cross = np.exp((idx[:, None] + 1.0) * lg[None, :])
    kd = np.exp((RET_BLOCK - 1.0 - idx)[:, None] * lg[None, :])
    cross = np.broadcast_to(cross.T[:, :, None], (HEADS, RET_BLOCK, 128))
    kd = kd.T[:, None, :]
    as_f32 = lambda a: jnp.asarray(np.asarray(a, np.float32))
    return as_f32(decay), as_f32(cross), as_f32(kd), as_f32(np.exp(RET_BLOCK * lg)), as_f32(np.exp(lg))


def kernel(x_prompt, x_sample, state_ret, ln_gain, ln_bias, w_in_ret, gn_gain_ret, w_out_ret,
           w_in_mlp, ln_gain_mlp, ln_bias_mlp, w_spatial, b_spatial, w_out_mlp):
    batch, seq, _ = x_prompt.shape
    n_dec = x_sample.shape[0]
    assert x_sample.shape[1] == 1 and seq % TILE == 0 and seq % MLP_TILE == 0

    lng0, lnb0 = ln_gain[0][None], ln_bias[0][None]
    lng1, lnb1 = ln_gain[1][None], ln_bias[1][None]
    gn_gain = gn_gain_ret[0][None]
    lng_m, lnb_m = ln_gain_mlp[0][None], ln_bias_mlp[0][None]

    rope = (*_rope_tables(np.arange(TILE)), *_rope_tables(TILE * np.arange(seq // TILE)))
    cos_s, sin_s = _rope_tables(PAST_LEN + np.arange(1))
    decay, cross, kd, gamma_chunk, gamma = _decay_tables()

    q, k, v, sg, oin, w_in0, w_out0 = _ret_sample_in(x_sample, cos_s, sin_s, w_in_ret[0], w_out_ret[0])

    x1, ret_state_prompt, w_in1, w_out1 = _ret_prompt(
        x_prompt.reshape(batch * seq, D_MODEL), rope, w_in0, decay, cross, kd, gamma_chunk,
        gn_gain, w_out0, lng0, lnb0, w_in_mlp[0], w_out_mlp[0], seq=seq)
    bsp = jnp.repeat(jnp.transpose(b_spatial[0]), 128, axis=1)
    y_prompt = _mlp_prompt(x1, w_in1, lng_m, lnb_m, w_spatial[0], bsp, w_out1, lng1, lnb1)
    ret_state_sample, oc = _state_sc(gamma, state_ret[0], q, k, v)

    wsd = jnp.repeat(w_spatial[0, :, 0, 0], GW)[None]
    bsd = jnp.repeat(b_spatial[0, :, 0], GW)[None]
    y_sample, mlp_v = _sample_finish(gamma, x_sample, oin, oc, sg, gn_gain, w_out0, lng0, lnb0,
                                     w_in1, lng_m, lnb_m, wsd, bsd, w_out1, lng1, lnb1)

    return (y_prompt.reshape(batch, seq, D_MODEL),
            y_sample,
            ret_state_prompt[None],
            ret_state_sample[None],
            mlp_v)
```

```python
import functools

import jax
import jax.numpy as jnp
import numpy as np
from jax import lax
from jax.experimental import pallas as pl
from jax.experimental.pallas import tpu as pltpu
from jax.experimental.pallas import tpu_sc as plsc

F32 = jnp.float32
BF16 = jnp.bfloat16

D_MODEL = 1024
DEPTH = 2
PAST_LEN = 16384
HEADS = 4
DK = D_MODEL // HEADS
DV = 2 * DK
QKW = HEADS * DK
VW = HEADS * DV
CHUNK = 128
ROPE_BASE = 10000.0
ROPE_HALF = DK // 2
MLP_W = 2 * D_MODEL
GROUPS = 8
GW = MLP_W // GROUPS
ALPHA = (2 * DEPTH) ** 0.25
LN_EPS = 1e-5
SQRT_HALF = float(np.sqrt(0.5))

TILE = 256
RET_BLOCK = TILE
MLP_TILE = 256
RET_STAGE_ORDER = ("qk0", "qk1", "qk2", "qk3", "ret0", "ret1", "ret2", "ret3", "out",
                   "gate0", "gate1", "gate2", "gate3", "v0", "v1", "v2", "v3")
MLP_STAGE_ORDER = ("gate0", "v0", "mix0", "gate1", "v1", "mix1", "gate2", "v2", "out", "gate3", "v3")
VMEM_LIMIT = 60 * 1024 * 1024


def _dot(a, b):
    return jnp.dot(a, b, preferred_element_type=F32)


def _gelu(x):
    return 0.5 * x * (1.0 + lax.erf(x * SQRT_HALF))


def _silu(x):
    return x * jax.nn.sigmoid(x)


def _norm_rows(x):
    mu = jnp.mean(x, axis=-1, keepdims=True)
    d = x - mu
    var = jnp.mean(d * d, axis=-1, keepdims=True)
    return d * lax.rsqrt(var + LN_EPS)


def _rotate(h, cos, sin):
    x1 = h[:, :ROPE_HALF]
    x2 = h[:, ROPE_HALF:]
    return jnp.concatenate([x1 * cos - x2 * sin, x2 * cos + x1 * sin], axis=-1)


def _const_spec(*shape):
    return pl.BlockSpec(shape, lambda s: (0,) * len(shape), pipeline_mode=pl.Buffered(1))


def _ret_project_qk(h, xb, cos, sin, win_ref, q_scr, k_scr):
    qs = slice(h * DK, (h + 1) * DK)
    q_scr[:, qs] = _rotate(_dot(xb, win_ref[:, qs]), cos, sin).astype(BF16)
    ks = slice(QKW + h * DK, QKW + (h + 1) * DK)
    k_scr[qs, :] = (_rotate(_dot(xb, win_ref[:, ks]), cos, sin) * (DK ** -0.5)).T


def _ret_project_gate(h, xb, win_ref, sg_scr):
    vs = slice(h * DV, (h + 1) * DV)
    sg_scr[:, vs] = _silu(_dot(xb, win_ref[:, 2 * QKW + VW + h * DV:2 * QKW + VW + (h + 1) * DV]))


def _ret_project_v(h, xb, win_ref, v_scr):
    vs = slice(h * DV, (h + 1) * DV)
    v_scr[:, vs] = _dot(xb, win_ref[:, 2 * QKW + h * DV:2 * QKW + (h + 1) * DV]).astype(BF16)


def _ret_head(h, q_scr, k_scr, v_scr, sg_scr, yin_scr, st_ref, gl_ref, decay_ref, cross_ref, kd_ref, gn_ref):
    qs = slice(h * DK, (h + 1) * DK)
    vs = slice(h * DV, (h + 1) * DV)
    for c in range(TILE // RET_BLOCK):
        rows = slice(c * RET_BLOCK, (c + 1) * RET_BLOCK)
        qc = q_scr[rows, qs]
        kt = k_scr[qs, rows]
        sb = (_dot(qc, kt.astype(BF16)) * decay_ref[h]).astype(BF16)
        kdec = (kt * kd_ref[h]).astype(BF16)
        halves = []
        for half in range(2):
            cols = slice(half * (DV // 2), (half + 1) * (DV // 2))
            vc = v_scr[rows, h * DV + half * (DV // 2):h * DV + (half + 1) * (DV // 2)]
            state = st_ref[h, :, cols]
            halves.append(_dot(sb, vc)
                          + _dot(qc, state.astype(BF16)) * jnp.tile(cross_ref[h], (1, DV // 256)))
            st_ref[h, :, cols] = gl_ref[h] * state + _dot(kdec, vc)
        o = jnp.concatenate(halves, axis=-1)
        on = _norm_rows(o) * gn_ref[:, vs]
        yin_scr[rows, vs] = (sg_scr[rows, vs] * on).astype(BF16)


def _ret_out(x_ref, yin_scr, wout_ref, lng_ref, lnb_ref, y_ref):
    z = ALPHA * x_ref[...] + _dot(yin_scr[...], wout_ref[...])
    y_ref[...] = _norm_rows(z) * lng_ref[...] + lnb_ref[...]


def _ret_prompt_kernel(gl_ref, x_ref, xp_ref, cosr_ref, sinr_ref, cost_ref, sint_ref, win_ref,
                       decay_ref, cross_ref, kd_ref, gn_ref, wout_ref, lng_ref, lnb_ref, wa_ref, wb_ref,
                       y_ref, st_ref, wa16_ref, wb16_ref,
                       qa, ka, va, sga, qb, kb, vb, sgb, yin_scr, *, tiles, tiles_per_seq):
    s = pl.program_id(0)
    prev = jnp.maximum(s - 1, 0)
    tile_in_seq = jnp.minimum(s, tiles - 1) % tiles_per_seq

    wa16_ref[...] = wa_ref[...].astype(BF16)
    wb16_ref[...] = wb_ref[...].astype(BF16)

    @pl.when(prev % tiles_per_seq == 0)
    def _():
        st_ref[...] = jnp.zeros_like(st_ref)

    def step(cur, old):
        stages = {}
        if cur is not None:
            ct = cost_ref[pl.ds(tile_in_seq, 1), :]
            st = sint_ref[pl.ds(tile_in_seq, 1), :]
            cos = ct * cosr_ref[...] - st * sinr_ref[...]
            sin = st * cosr_ref[...] + ct * sinr_ref[...]
            xb = x_ref[...].astype(BF16)
            for h in range(HEADS):
                stages[f"qk{h}"] = functools.partial(_ret_project_qk, h, xb, cos, sin, win_ref, *cur[:2])
                stages[f"gate{h}"] = functools.partial(_ret_project_gate, h, xb, win_ref, cur[3])
                stages[f"v{h}"] = functools.partial(_ret_project_v, h, xb, win_ref, cur[2])
        if old is not None:
            for h in range(HEADS):
                stages[f"ret{h}"] = functools.partial(_ret_head, h, *old, yin_scr, st_ref, gl_ref, decay_ref,
                                                      cross_ref, kd_ref, gn_ref)
            stages["out"] = functools.partial(_ret_out, xp_ref, yin_scr, wout_ref, lng_ref, lnb_ref, y_ref)
        assert set(stages) <= set(RET_STAGE_ORDER)
        for name in RET_STAGE_ORDER:
            if name in stages:
                stages[name]()

    sets = ((qa, ka, va, sga), (qb, kb, vb, sgb))

    @pl.when(s == 0)
    def _():
        step(sets[0], None)

    for parity in range(2):
        @pl.when(jnp.logical_and(jnp.logical_and(s > 0, s < tiles), s % 2 == parity))
        def _():
            step(sets[parity], sets[1 - parity])

    @pl.when(s == tiles)
    def _():
        step(None, sets[(tiles - 1) % 2])


def _ret_prompt(x, rope, w_in, decay, cross, kd, gl, gn_gain, w_out, ln_g, ln_b, w_next_a, w_next_b, *, seq):
    cos_r, sin_r, cos_t, sin_t = rope
    n = x.shape[0]
    tiles = n // TILE
    tiles_per_seq = seq // TILE
    cur = lambda s: jnp.minimum(s, tiles - 1)
    prev = lambda s: jnp.maximum(s - 1, 0)
    slab_a = w_next_a.shape[0] // tiles
    slab_b = w_next_b.shape[0] // tiles
    assert slab_a * tiles == w_next_a.shape[0] and slab_b * tiles == w_next_b.shape[0]
    return pl.pallas_call(
        functools.partial(_ret_prompt_kernel, tiles=tiles, tiles_per_seq=tiles_per_seq),
        grid=(tiles + 1,),
        in_specs=[
            pl.BlockSpec(memory_space=pltpu.SMEM),
            pl.BlockSpec((TILE, D_MODEL), lambda s: (cur(s), 0)),
            pl.BlockSpec((TILE, D_MODEL), lambda s: (prev(s), 0)),
            _const_spec(TILE, ROPE_HALF),
            _const_spec(TILE, ROPE_HALF),
            _const_spec(tiles_per_seq, ROPE_HALF),
            _const_spec(tiles_per_seq, ROPE_HALF),
            _const_spec(D_MODEL, 2 * QKW + 2 * VW),
            _const_spec(HEADS, RET_BLOCK, RET_BLOCK),
            _const_spec(HEADS, RET_BLOCK, 128),
            _const_spec(HEADS, 1, RET_BLOCK),
            _const_spec(1, VW),
            _const_spec(VW, D_MODEL),
            _const_spec(1, D_MODEL),
            _const_spec(1, D_MODEL),
            pl.BlockSpec((slab_a, w_next_a.shape[1]), lambda s: (cur(s), 0)),
            pl.BlockSpec((slab_b, w_next_b.shape[1]), lambda s: (cur(s), 0)),
        ],
        out_specs=[
            pl.BlockSpec((TILE, D_MODEL), lambda s: (prev(s), 0)),
            pl.BlockSpec((None, HEADS, DK, DV), lambda s: (prev(s) // tiles_per_seq, 0, 0, 0)),
            pl.BlockSpec((slab_a, w_next_a.shape[1]), lambda s: (cur(s), 0)),
            pl.BlockSpec((slab_b, w_next_b.shape[1]), lambda s: (cur(s), 0)),
        ],
        out_shape=[
            jax.ShapeDtypeStruct((n, D_MODEL), F32),
            jax.ShapeDtypeStruct((n // seq, HEADS, DK, DV), F32),
            jax.ShapeDtypeStruct(w_next_a.shape, BF16),
            jax.ShapeDtypeStruct(w_next_b.shape, BF16),
        ],
        scratch_shapes=2 * [
            pltpu.VMEM((TILE, QKW), BF16),
            pltpu.VMEM((QKW, TILE), F32),
            pltpu.VMEM((TILE, VW), BF16),
            pltpu.VMEM((TILE, VW), F32),
        ] + [pltpu.VMEM((TILE, VW), BF16)],
        compiler_params=pltpu.CompilerParams(
            dimension_semantics=("arbitrary",),
            vmem_limit_bytes=VMEM_LIMIT),
        name="ret_prompt",
    )(gl, x, x, cos_r, sin_r, cos_t, sin_t, w_in, decay, cross, kd, gn_gain, w_out, ln_g, ln_b,
      w_next_a, w_next_b)


def _mlp_project_gate(j, xb, win_ref, p_scr):
    u = _dot(xb, win_ref[:, j * DV:(j + 1) * DV])
    g = _dot(xb, win_ref[:, 2 * MLP_W + j * DV:2 * MLP_W + (j + 1) * DV])
    p_scr[:, j * DV:(j + 1) * DV] = _gelu(u) * _silu(g)


def _mlp_project_v(j, xb, win_ref, gv_scr):
    gv_scr[:, j * DV:(j + 1) * DV] = _gelu(_dot(xb, win_ref[:, MLP_W + j * DV:MLP_W + (j + 1) * DV]))


def _causal_mix_weights(ws_ref):
    row = lax.broadcasted_iota(jnp.int32, (CHUNK, CHUNK), 0)
    col = lax.broadcasted_iota(jnp.int32, (CHUNK, CHUNK), 1)
    return [jnp.where(row >= col, ws_ref[g], 0.0).astype(BF16) for g in range(GROUPS)]


def _mlp_mix(c, ws, p_scr, gv_scr, yin_scr, lngm_ref, lnbm_ref, bsp_ref):
    rows = slice(c * CHUNK, (c + 1) * CHUNK)
    vn = (_norm_rows(gv_scr[rows, :]) * lngm_ref[...] + lnbm_ref[...]).astype(BF16)
    for g in range(GROUPS):
        gs = slice(g * GW, (g + 1) * GW)
        bias = jnp.tile(bsp_ref[:, g * 128:(g + 1) * 128], (1, GW // 128))
        mixed = _dot(ws[g], vn[:, gs]) + bias
        yin_scr[rows, gs] = (p_scr[rows, gs] * mixed).astype(BF16)


def _mlp_out(x_ref, yin_scr, wout_ref, lng_ref, lnb_ref, y_ref):
    z = ALPHA * x_ref[...] + _dot(yin_scr[...], wout_ref[...])
    y_ref[...] = _norm_rows(z) * lng_ref[...] + lnb_ref[...]


def _mlp_prompt_kernel(x_ref, xp_ref, win_ref, lngm_ref, lnbm_ref, ws_ref, bsp_ref, wout_ref,
                       lng_ref, lnb_ref,
                       y_ref,
                       pa, gva, pb, gvb, yin_scr, *, tiles):
    s = pl.program_id(0)

    def step(cur, old):
        stages = {}
        if cur is not None:
            xb = x_ref[...].astype(BF16)
            for j in range(MLP_W // DV):
                stages[f"gate{j}"] = functools.partial(_mlp_project_gate, j, xb, win_ref, cur[0])
                stages[f"v{j}"] = functools.partial(_mlp_project_v, j, xb, win_ref, cur[1])
        if old is not None:
            ws = _causal_mix_weights(ws_ref)
            for c in range(MLP_TILE // CHUNK):
                stages[f"mix{c}"] = functools.partial(_mlp_mix, c, ws, *old, yin_scr, lngm_ref, lnbm_ref,
                                                      bsp_ref)
            stages["out"] = functools.partial(_mlp_out, xp_ref, yin_scr, wout_ref, lng_ref, lnb_ref, y_ref)
        assert set(stages) <= set(MLP_STAGE_ORDER)
        for name in MLP_STAGE_ORDER:
            if name in stages:
                stages[name]()

    sets = ((pa, gva), (pb, gvb))

    @pl.when(s == 0)
    def _():
        step(sets[0], None)

    for parity in range(2):
        @pl.when(jnp.logical_and(jnp.logical_and(s > 0, s < tiles), s % 2 == parity))
        def _():
            step(sets[parity], sets[1 - parity])

    @pl.when(s == tiles)
    def _():
        step(None, sets[(tiles - 1) % 2])


def _mlp_prompt(x, w_in, lng_m, lnb_m, w_s, bsp, w_out, ln_g, ln_b):
    n = x.shape[0]
    tiles = n // MLP_TILE
    cur = lambda s: jnp.minimum(s, tiles - 1)
    prev = lambda s: jnp.maximum(s - 1, 0)
    return pl.pallas_call(
        functools.partial(_mlp_prompt_kernel, tiles=tiles),
        grid=(tiles + 1,),
        in_specs=[
            pl.BlockSpec((MLP_TILE, D_MODEL), lambda s: (cur(s), 0)),
            pl.BlockSpec((MLP_TILE, D_MODEL), lambda s: (prev(s), 0)),
            _const_spec(D_MODEL, 3 * MLP_W),
            _const_spec(1, MLP_W),
            _const_spec(1, MLP_W),
            _const_spec(GROUPS, CHUNK, CHUNK),
            _const_spec(CHUNK, GROUPS * 128),
            _const_spec(MLP_W, D_MODEL),
            _const_spec(1, D_MODEL),
            _const_spec(1, D_MODEL),
        ],
        out_specs=pl.BlockSpec((MLP_TILE, D_MODEL), lambda s: (prev(s), 0)),
        out_shape=jax.ShapeDtypeStruct((n, D_MODEL), F32),
        scratch_shapes=2 * [
            pltpu.VMEM((MLP_TILE, MLP_W), F32),
            pltpu.VMEM((MLP_TILE, MLP_W), F32),
        ] + [pltpu.VMEM((MLP_TILE, MLP_W), BF16)],
        compiler_params=pltpu.CompilerParams(
            dimension_semantics=("arbitrary",),
            vmem_limit_bytes=VMEM_LIMIT),
        name="mlp_prompt",
    )(x, x, w_in, lng_m, lnb_m, w_s, bsp, w_out, ln_g, ln_b)


SAMPLE_COLS = 1024


def _ret_sample_in_kernel(x_ref, cos_ref, sin_ref, win_ref, wo_ref,
                          q_ref, k_ref, v_ref, sg_ref, oin_ref, win16_ref, wo16_ref, xb_scr):
    i = pl.program_id(0)

    @pl.when(i == 0)
    def _():
        xb_scr[...] = x_ref[:, 0, :].astype(BF16)

    w16 = win_ref[...].astype(BF16)
    win16_ref[...] = w16
    wo16_ref[...] = wo_ref[...].astype(BF16)
    h = _dot(xb_scr[...], w16)
    cos = cos_ref[...]
    sin = sin_ref[...]
    q_steps = QKW // SAMPLE_COLS
    v_steps = VW // SAMPLE_COLS
    qk_heads = SAMPLE_COLS // DK
    v_heads = SAMPLE_COLS // DV

    @pl.when(i < q_steps)
    def _():
        for j in range(qk_heads):
            q_ref[i * qk_heads + j] = _rotate(h[:, j * DK:(j + 1) * DK], cos, sin)

    @pl.when(jnp.logical_and(i >= q_steps, i < 2 * q_steps))
    def _():
        for j in range(qk_heads):
            k_ref[(i - q_steps) * qk_heads + j] = _rotate(h[:, j * DK:(j + 1) * DK], cos, sin) * (DK ** -0.5)

    @pl.when(jnp.logical_and(i >= 2 * q_steps, i < 2 * q_steps + v_steps))
    def _():
        for j in range(v_heads):
            v_ref[(i - 2 * q_steps) * v_heads + j] = h[:, j * DV:(j + 1) * DV]

    @pl.when(i >= 2 * q_steps + v_steps)
    def _():
        for j in range(v_heads):
            sg_ref[(i - 2 * q_steps - v_steps) * v_heads + j] = _silu(h[:, j * DV:(j + 1) * DV])

    @pl.when(i == pl.num_programs(0) - 1)
    def _():
        for hd in range(HEADS):
            qk = jnp.sum(q_ref[hd].astype(BF16).astype(F32) * k_ref[hd].astype(BF16).astype(F32),
                         axis=-1, keepdims=True)
            oin_ref[hd] = qk.astype(BF16).astype(F32) * v_ref[hd].astype(BF16).astype(F32)


def _ret_sample_in(x, cos, sin, w_in, w_out):
    n = x.shape[0]
    steps = w_in.shape[1] // SAMPLE_COLS
    wo_rows = 512
    wo_steps = w_out.shape[0] // wo_rows
    assert QKW % SAMPLE_COLS == 0 and SAMPLE_COLS % DV == 0 and wo_steps <= steps
    whole = lambda *shape: pl.BlockSpec(shape, lambda i: (0,) * len(shape))
    wo_spec = pl.BlockSpec((wo_rows, w_out.shape[1]), lambda i: (jnp.minimum(i, wo_steps - 1), 0))
    return pl.pallas_call(
        _ret_sample_in_kernel,
        grid=(steps,),
        in_specs=[
            whole(n, 1, D_MODEL),
            whole(1, ROPE_HALF),
            whole(1, ROPE_HALF),
            pl.BlockSpec((D_MODEL, SAMPLE_COLS), lambda i: (0, i)),
            wo_spec,
        ],
        out_specs=[
            whole(HEADS, n, DK),
            whole(HEADS, n, DK),
            whole(HEADS, n, DV),
            whole(HEADS, n, DV),
            whole(HEADS, n, DV),
            pl.BlockSpec((D_MODEL, SAMPLE_COLS), lambda i: (0, i)),
            wo_spec,
        ],
        out_shape=[
            jax.ShapeDtypeStruct((HEADS, n, DK), F32),
            jax.ShapeDtypeStruct((HEADS, n, DK), F32),
            jax.ShapeDtypeStruct((HEADS, n, DV), F32),
            jax.ShapeDtypeStruct((HEADS, n, DV), F32),
            jax.ShapeDtypeStruct((HEADS, n, DV), F32),
            jax.ShapeDtypeStruct(w_in.shape, BF16),
            jax.ShapeDtypeStruct(w_out.shape, BF16),
        ],
        scratch_shapes=[pltpu.VMEM((n, D_MODEL), BF16)],
        compiler_params=pltpu.CompilerParams(
            dimension_semantics=("arbitrary",),
            vmem_limit_bytes=VMEM_LIMIT),
        name="ret_sample_in",
    )(x, cos, sin, w_in, w_out)


SC_LANES = 16
SC_ROWS = 32
SC_HALF = 256


def _state_sc_kernel(gam_hbm, s0_hbm, k_hbm, q_hbm, v_hbm, s1_hbm, o_hbm,
                     gam_v, v_v, o_v, k_v, q_v, s0_v, s1_v, *, pairs_per_worker, num_cores, n):
    wid = lax.axis_index("s") * num_cores + lax.axis_index("c")
    first = wid * pairs_per_worker
    head = first // n
    pltpu.sync_copy(gam_hbm, gam_v)
    pltpu.sync_copy(v_hbm.at[pl.ds(first, pairs_per_worker)], v_v)
    pltpu.sync_copy(k_hbm.at[pl.ds(first, pairs_per_worker)], k_v)
    pltpu.sync_copy(q_hbm.at[pl.ds(first, pairs_per_worker)], q_v)
    gam = gam_v[head, pl.ds(0, SC_LANES)]

    @pl.loop(0, pairs_per_worker)
    def _(pi):
        seq = first + pi - head * n
        pi_vec = jnp.full((SC_LANES,), pi, jnp.int32)
        for c in range(DV // SC_LANES):
            o_v[pi, pl.ds(c * SC_LANES, SC_LANES)] = jnp.zeros((SC_LANES,), F32)

        @pl.loop(0, DK // SC_ROWS)
        def _(blk):
            row0 = (seq * HEADS + head) * DK + blk * SC_ROWS
            pltpu.sync_copy(s0_hbm.at[pl.ds(row0, SC_ROWS)], s0_v)
            for half in range(DV // SC_HALF):
                cols = [half * SC_HALF + c * SC_LANES for c in range(SC_HALF // SC_LANES)]
                vregs = [v_v[pi, pl.ds(c, SC_LANES)] for c in cols]

                @pl.loop(0, SC_ROWS // 8)
                def _(grp):
                    acc = [None] * len(cols)
                    for j in range(8):
                        d = grp * 8 + j
                        d_vec = jnp.full((SC_LANES,), blk * SC_ROWS + d, jnp.int32)
                        kv = plsc.load_gather(k_v, [pi_vec, d_vec])
                        qv = plsc.load_gather(q_v, [pi_vec, d_vec])
                        for i, (c, vreg) in enumerate(zip(cols, vregs)):
                            s0 = s0_v[d, pl.ds(c, SC_LANES)]
                            s1_v[d, pl.ds(c, SC_LANES)] = gam * s0 + kv * vreg
                            acc[i] = qv * s0 if j == 0 else acc[i] + qv * s0
                    for c, a in zip(cols, acc):
                        plsc.addupdate(o_v.at[pi, pl.ds(c, SC_LANES)], a)
            pltpu.sync_copy(s1_v, s1_hbm.at[pl.ds(row0, SC_ROWS)])

    pltpu.sync_copy(o_v, o_hbm.at[pl.ds(first, pairs_per_worker)])


def _state_sc(gamma, s0, q, k, v):
    n = s0.shape[0]
    info = plsc.get_sparse_core_info()
    workers = info.num_cores * info.num_subcores
    pairs = n * HEADS
    pairs_per_worker = pairs // workers
    assert info.num_lanes == SC_LANES and pairs % workers == 0 and n % pairs_per_worker == 0
    gam_rows = jnp.broadcast_to(jnp.pad(gamma, (0, 8 - HEADS))[:, None], (8, 128))
    mesh = plsc.VectorSubcoreMesh(core_axis_name="c", subcore_axis_name="s")
    s1, o = pl.kernel(
        functools.partial(_state_sc_kernel, pairs_per_worker=pairs_per_worker, num_cores=info.num_cores, n=n),
        out_type=[
            jax.ShapeDtypeStruct((pairs * DK, DV), F32),
            jax.ShapeDtypeStruct((pairs, DV), F32),
        ],
        mesh=mesh,
        scratch_types=[
            pltpu.VMEM((8, 128), F32),
            pltpu.VMEM((pairs_per_worker, DV), F32),
            pltpu.VMEM((pairs_per_worker, DV), F32),
            pltpu.VMEM((pairs_per_worker, DK), F32),
            pltpu.VMEM((pairs_per_worker, DK), F32),
            pltpu.VMEM((SC_ROWS, DV), F32),
            pltpu.VMEM((SC_ROWS, DV), F32),
        ],
        compiler_params=pltpu.CompilerParams(use_tc_tiling_on_sc=True, needs_layout_passes=False),
        name="state_sc",
    )(gam_rows, s0.reshape(pairs * DK, DV), k.reshape(pairs, DK), q.reshape(pairs, DK), v.reshape(pairs, DV))
    return s1.reshape(s0.shape), o.reshape(HEADS, n, DV)


def _sample_finish_kernel(gam_ref, x_ref, oin_ref, oc_ref, sg_ref, gn_ref, wout0_ref, lng0_ref, lnb0_ref,
                          win_ref, lngm_ref, lnbm_ref, wsd_ref, bsd_ref, wout1_ref, lng1_ref, lnb1_ref,
                          y_ref, vout_ref, yin_scr):
    x = x_ref[:, 0, :]
    for h in range(HEADS):
        vs = slice(h * DV, (h + 1) * DV)
        o = oin_ref[h] + oc_ref[h] * gam_ref[h]
        yin_scr[:, vs] = (sg_ref[h] * (_norm_rows(o) * gn_ref[:, vs])).astype(BF16)
    z = ALPHA * x + _dot(yin_scr[...], wout0_ref[...])
    x1 = _norm_rows(z) * lng0_ref[...] + lnb0_ref[...]

    xb = x1.astype(BF16)
    u = _gelu(_dot(xb, win_ref[:, :MLP_W]))
    v = _gelu(_dot(xb, win_ref[:, MLP_W:2 * MLP_W]))
    g = _dot(xb, win_ref[:, 2 * MLP_W:])
    vn = _norm_rows(v) * lngm_ref[...] + lnbm_ref[...]
    vout_ref[0, :, 0, :] = vn
    mixed = wsd_ref[...].astype(BF16).astype(F32) * vn.astype(BF16).astype(F32) + bsd_ref[...]
    yin = (u * mixed * _silu(g)).astype(BF16)
    z1 = ALPHA * x1 + _dot(yin, wout1_ref[...])
    y_ref[:, 0, :] = _norm_rows(z1) * lng1_ref[...] + lnb1_ref[...]


def _sample_finish(gamma, x, oin, oc, sg, gn_gain, w_out0, lng0, lnb0,
                   w_in1, lng_m, lnb_m, wsd, bsd, w_out1, lng1, lnb1):
    n = x.shape[0]
    vmem = pl.BlockSpec(memory_space=pltpu.VMEM)
    return pl.pallas_call(
        _sample_finish_kernel,
        in_specs=[pl.BlockSpec(memory_space=pltpu.SMEM)] + [vmem] * 16,
        out_shape=[
            jax.ShapeDtypeStruct((n, 1, D_MODEL), F32),
            jax.ShapeDtypeStruct((1, n, 1, MLP_W), F32),
        ],
        scratch_shapes=[pltpu.VMEM((n, VW), BF16)],
        compiler_params=pltpu.CompilerParams(vmem_limit_bytes=VMEM_LIMIT),
        name="sample_finish",
    )(gamma, x, oin, oc, sg, gn_gain, w_out0, lng0, lnb0,
      w_in1, lng_m, lnb_m, wsd, bsd, w_out1, lng1, lnb1)


def _rope_tables(pos):
    inv = (np.float32(ROPE_BASE) ** (-np.arange(ROPE_HALF, dtype=np.float32) / np.float32(ROPE_HALF)))
    ang = np.asarray(pos, np.float32)[:, None] * inv.astype(np.float32)[None, :]
    return jnp.asarray(np.cos(ang), F32), jnp.asarray(np.sin(ang), F32)


def _decay_tables():
    lg = np.log1p(-np.exp2(-5.0 - np.arange(HEADS, dtype=np.float32))).astype(np.float32)
    idx = np.arange(RET_BLOCK, dtype=np.float32)
    diff = idx[:, None] - idx[None, :]
    decay = np.where(diff >= 0, np.exp(np.maximum(diff, 0.0)[None] * lg[:, None, None]), 0.0)
    cross = np.exp((idx[:, None] + 1.0) * lg[None, :])
    kd = np.exp((RET_BLOCK - 1.0 - idx)[:, None] * lg[None, :])
    cross = np.broadcast_to(cross.T[:, :, None], (HEADS, RET_BLOCK, 128))
    kd = kd.T[:, None, :]
    as_f32 = lambda a: jnp.asarray(np.asarray(a, np.float32))
    return as_f32(decay), as_f32(cross), as_f32(kd), as_f32(np.exp(RET_BLOCK * lg)), as_f32(np.exp(lg))


def kernel(x_prompt, x_sample, state_ret, ln_gain, ln_bias, w_in_ret, gn_gain_ret, w_out_ret,
           w_in_mlp, ln_gain_mlp, ln_bias_mlp, w_spatial, b_spatial, w_out_mlp):
    batch, seq, _ = x_prompt.shape
    n_dec = x_sample.shape[0]
    assert x_sample.shape[1] == 1 and seq % TILE == 0 and seq % MLP_TILE == 0

    lng0, lnb0 = ln_gain[0][None], ln_bias[0][None]
    lng1, lnb1 = ln_gain[1][None], ln_bias[1][None]
    gn_gain = gn_gain_ret[0][None]
    lng_m, lnb_m = ln_gain_mlp[0][None], ln_bias_mlp[0][None]

    rope = (*_rope_tables(np.arange(TILE)), *_rope_tables(TILE * np.arange(seq // TILE)))
    cos_s, sin_s = _rope_tables(PAST_LEN + np.arange(1))
    decay, cross, kd, gamma_chunk, gamma = _decay_tables()

    q, k, v, sg, oin, w_in0, w_out0 = _ret_sample_in(x_sample, cos_s, sin_s, w_in_ret[0], w_out_ret[0])

    x1, ret_state_prompt, w_in1, w_out1 = _ret_prompt(
        x_prompt.reshape(batch * seq, D_MODEL), rope, w_in0, decay, cross, kd, gamma_chunk,
        gn_gain, w_out0, lng0, lnb0, w_in_mlp[0], w_out_mlp[0], seq=seq)
    bsp = jnp.repeat(jnp.transpose(b_spatial[0]), 128, axis=1)
    y_prompt = _mlp_prompt(x1, w_in1, lng_m, lnb_m, w_spatial[0], bsp, w_out1, lng1, lnb1)
    ret_state_sample, oc = _state_sc(gamma, state_ret[0], q, k, v)

    wsd = jnp.repeat(w_spatial[0, :, 0, 0], GW)[None]
    bsd = jnp.repeat(b_spatial[0, :, 0], GW)[None]
    y_sample, mlp_v = _sample_finish(gamma, x_sample, oin, oc, sg, gn_gain, w_out0, lng0, lnb0,
                                     w_in1, lng_m, lnb_m, wsd, bsd, w_out1, lng1, lnb1)

    return (y_prompt.reshape(batch, seq, D_MODEL),
            y_sample,
            ret_state_prompt[None],
            ret_state_sample[None],
            mlp_v)
```
